```python
import jax
import jax.numpy as jnp
from jax import lax
import numpy as np

D_MODEL = 2048
BATCH = 8
SEQ = 2048
DEPTH = 2

GRID_W = 64
CTX_LEN = 256

HA = 8
DK_A = 128
DV_A = 128
WK_A = HA * DK_A
WA = HA * DV_A
CHUNK_A = 64

POOL_WINDOWS = (2, 4, 8, 16)
N_POOL = 4
POOL_GW = 256
WB = N_POOL * POOL_GW

HQ = 8
HKV = 2
GQ = HQ // HKV
HD = 128
WC = HQ * HD
WINDOW = 128
BLOCK_Q = 128
ROPE_THETA = 10000.0

N_BRANCH = 3
BRANCH_W = 1024
D_FF = 4 * D_MODEL
N_ADA = 6
LN_EPS = 1e-5
RMS_EPS = 1e-6
ALPHA = (2.0 * DEPTH) ** 0.25
BETA = (8.0 * DEPTH) ** -0.25

COL_AQ = 0
COL_AI = COL_AQ + WK_A
COL_AG = COL_AI + WA
COL_AFF = COL_AG + WA
COL_AFB = COL_AFF + WK_A
COL_B = COL_AFB + WK_A
COL_CQ = COL_B + WB
COL_CK = COL_CQ + WC
COL_CV = COL_CK + HKV * HD
COL_GATE = COL_CV + HKV * HD
N_COLS = COL_GATE + N_BRANCH * D_MODEL

kernel_name = 'hybrid_dit_hgrn2_pool_swa_trunk'


def _layer_norm(x, g, b):
    xf = x.astype(jnp.float32)
    mu = jnp.mean(xf, -1, keepdims=True)
    var = jnp.mean(jnp.square(xf - mu), -1, keepdims=True)
    return ((xf - mu) * lax.rsqrt(var + LN_EPS)).astype(x.dtype) * g + b


def _head_rms_norm(o, g):
    B, T = o.shape[:2]
    o = o * lax.rsqrt(jnp.mean(jnp.square(o), -1, keepdims=True) + RMS_EPS)
    return o.reshape(B, T, -1).astype(g.dtype) * g


def _axial_rope(t_len):
    rows = t_len // GRID_W
    row = jnp.repeat(jnp.arange(rows, dtype=jnp.float32), GRID_W)
    col = jnp.tile(jnp.arange(GRID_W, dtype=jnp.float32), rows)
    n_freq = HD // 4
    inv_freq = ROPE_THETA ** (-jnp.arange(n_freq, dtype=jnp.float32) / n_freq)
    ang = jnp.concatenate([row[:, None] * inv_freq, col[:, None] * inv_freq], -1)
    return jnp.cos(ang), jnp.sin(ang)


def _apply_rope(x, cos, sin):
    shape = (1, x.shape[1]) + (1,) * (x.ndim - 3) + (HD // 2,)
    cos = cos.reshape(shape).astype(x.dtype)
    sin = sin.reshape(shape).astype(x.dtype)
    x1, x2 = x[..., : HD // 2], x[..., HD // 2:]
    return jnp.concatenate([x1 * cos - x2 * sin, x2 * cos + x1 * sin], -1)


def _hgrn_gates(z, lb):
    z = z.astype(jnp.float32)
    lb = lb.astype(jnp.float32)
    logf = jnp.logaddexp(jnp.log(lb), jnp.log1p(-lb) + jax.nn.log_sigmoid(z))
    k = (1.0 - lb) * jax.nn.sigmoid(-z)
    return logf, k


def _hgrn_chunk_scan(q, k, logf, v, s0):
    B, T, H, _ = q.shape
    n = T // CHUNK_A

    def chunks(a):
        a = a.astype(jnp.float32).reshape(B, n, CHUNK_A, H, a.shape[-1])
        return jnp.transpose(a, (1, 0, 3, 2, 4))

    lower = jnp.tril(jnp.ones((CHUNK_A, CHUNK_A), dtype=bool))[:, :, None]

    def step(s, inp):
        qc, kc, lfc, vc = inp
        b = jnp.cumsum(lfc, axis=-2)
        diff = b[..., :, None, :] - b[..., None, :, :]
        decay = jnp.exp(jnp.where(lower, diff, -jnp.inf))
        scores = jnp.einsum('bhtd,bhsd,bhtsd->bhts', qc, kc, decay)
        o = jnp.einsum('bhts,bhsv->bhtv', scores, vc) + jnp.einsum('bhtd,bhdv->bhtv', qc * jnp.exp(b), s)
        b_end = b[..., -1:, :]
        s_new = jnp.exp(b_end[..., 0, :])[..., None] * s + jnp.einsum('bhsd,bhsv->bhdv', kc * jnp.exp(b_end - b), vc)
        return s_new, o

    s_fin, o = lax.scan(step, s0, (chunks(q), chunks(k), chunks(logf), chunks(v)))
    o = jnp.transpose(o, (1, 0, 3, 2, 4)).reshape(B, T, H, -1)
    return o, s_fin


def _hgrn_split(u, lb_fwd, lb_bwd):
    B, T = u.shape[:2]
    q = u[..., COL_AQ:COL_AQ + WK_A].reshape(B, T, HA, DK_A)
    i = u[..., COL_AI:COL_AI + WA].reshape(B, T, HA, DV_A)
    g = u[..., COL_AG:COL_AG + WA]
    lf_f, k_f = _hgrn_gates(u[..., COL_AFF:COL_AFF + WK_A].reshape(B, T, HA, DK_A), lb_fwd.reshape(HA, DK_A))
    lf_b, k_b = _hgrn_gates(u[..., COL_AFB:COL_AFB + WK_A].reshape(B, T, HA, DK_A), lb_bwd.reshape(HA, DK_A))
    return q, i, g, lf_f, k_f, lf_b, k_b


def _hgrn_mixer(u_c, u_l, lb_fwd, lb_bwd, norm_g, need_ctx):
    qc, ic, gc, lfc_f, kc_f, lfc_b, kc_b = _hgrn_split(u_c, lb_fwd, lb_bwd)
    ql, il, gl, lfl_f, kl_f, lfl_b, kl_b = _hgrn_split(u_l, lb_fwd, lb_bwd)
    B = u_l.shape[0]
    s0 = jnp.zeros((B, HA, DK_A, DV_A), jnp.float32)
    flip = lambda a: jnp.flip(a, axis=1)
    oc_f, sc_f = _hgrn_chunk_scan(qc, kc_f, lfc_f, ic, s0)
    oc_b, sc_b = _hgrn_chunk_scan(flip(qc), flip(kc_b), flip(lfc_b), flip(ic), s0)
    ol_f, _ = _hgrn_chunk_scan(ql, kl_f, lfl_f, il, sc_f)
    ol_b, _ = _hgrn_chunk_scan(flip(ql), flip(kl_b), flip(lfl_b), flip(il), sc_b)
    y_l = _head_rms_norm(ol_f + flip(ol_b), norm_g) * jax.nn.silu(gl)
    y_c = _head_rms_norm(oc_f + flip(oc_b), norm_g) * jax.nn.silu(gc) if need_ctx else None
    return y_c, y_l


def _centred_mean(u, w):
    T = u.shape[1]
    cs = jnp.pad(jnp.cumsum(u.astype(jnp.float32), axis=1), ((0, 0), (1, 0), (0, 0)))
    t = jnp.arange(T)
    lo = jnp.clip(t - w // 2, 0, T)
    hi = jnp.clip(t + w // 2, 0, T)
    cnt = (hi - lo).astype(jnp.float32)[None, :, None]
    return ((cs[:, hi] - cs[:, lo]) / cnt).astype(u.dtype)


def _pool_mixer(u, w_grp, scale):
    B, T = u.shape[:2]
    ug = u[..., COL_B:COL_B + WB].reshape(B, T, N_POOL, POOL_GW)
    pooled = jnp.stack([_centred_mean(ug[:, :, j], w) for j, w in enumerate(POOL_WINDOWS)], axis=2)
    y = jnp.einsum('btgc,gcd->btgd', pooled - ug, w_grp).reshape(B, T, WB)
    return y * scale


def _attention_mixer(u_c, u_l, sink, need_ctx):
    B, T = u_l.shape[:2]
    nb = T // BLOCK_Q
    nw = 3 * BLOCK_Q

    def qkv(u):
        Tn = u.shape[1]
        q = u[..., COL_CQ:COL_CQ + WC].reshape(B, Tn, HKV, GQ, HD)
        k = u[..., COL_CK:COL_CK + HKV * HD].reshape(B, Tn, HKV, HD)
        v = u[..., COL_CV:COL_CV + HKV * HD].reshape(B, Tn, HKV, HD)
        return q, k, v

    qc, kc, vc = qkv(u_c)
    ql, kl, vl = qkv(u_l)
    nc = kc.shape[1]
    cos, sin = _axial_rope(T)
    ql = _apply_rope(ql, cos, sin)
    kl = _apply_rope(kl, cos, sin)
    scale = HD ** -0.5
    sink_l = sink.astype(jnp.float32).reshape(HKV, GQ)

    qb = ql.reshape(B, nb, BLOCK_Q, HKV, GQ, HD)

    def band(a):
        ap = jnp.pad(a, ((0, 0), (BLOCK_Q, BLOCK_Q), (0, 0), (0, 0))).reshape(B, nb + 2, BLOCK_Q, HKV, HD)
        return jnp.concatenate([ap[:, :-2], ap[:, 1:-1], ap[:, 2:]], axis=2)

    kw, vw = band(kl), band(vl)
    q_pos = jnp.arange(T).reshape(nb, BLOCK_Q)
    k_pos = jnp.arange(nb)[:, None] * BLOCK_Q + jnp.arange(nw)[None, :] - BLOCK_Q
    kp = k_pos[:, None, :]
    valid = (jnp.abs(q_pos[:, :, None] - kp) <= WINDOW) & (kp >= 0) & (kp < T)
    s_win = jnp.einsum('bnqkgd,bnskd->bnkgqs', qb, kw).astype(jnp.float32) * scale
    s_win = jnp.where(valid[None, :, None, None], s_win, -jnp.inf)
    s_ctx = jnp.einsum('bnqkgd,bskd->bnkgqs', qb, kc).astype(jnp.float32) * scale
    s_sink = jnp.broadcast_to(sink_l[None, None, :, :, None, None], s_ctx.shape[:-1] + (1,))
    p = jax.nn.softmax(jnp.concatenate([s_win, s_ctx, s_sink], -1), axis=-1).astype(u_l.dtype)
    o = jnp.einsum('bnkgqs,bnskd->bnqkgd', p[..., :nw], vw) + jnp.einsum('bnkgqs,bskd->bnqkgd', p[..., nw:nw + nc], vc)
    y_l = o.reshape(B, T, WC)

    y_c = None
    if need_ctx:
        s_cc = jnp.einsum('btkgd,bskd->bkgts', qc, kc).astype(jnp.float32) * scale
        s_sink_c = jnp.broadcast_to(sink_l[None, :, :, None, None], s_cc.shape[:-1] + (1,))
        p_c = jax.nn.softmax(jnp.concatenate([s_cc, s_sink_c], -1), axis=-1).astype(u_c.dtype)
        y_c = jnp.einsum('bkgts,bskd->btkgd', p_c[..., :nc], vc).reshape(B, nc, WC)
    return y_c, y_l


def _merge_branches(u, ys, w_branch, w_out):
    B, T = u.shape[:2]
    z = jnp.einsum('btjc,jcd->btjd', jnp.stack(ys, axis=2), w_branch)
    gates = jax.nn.sigmoid(u[..., COL_GATE:].reshape(B, T, N_BRANCH, D_MODEL))
    return jnp.sum(gates * z, axis=2) @ w_out


def _sq_relu_mlp(h, w1, w2):
    return jnp.square(jax.nn.relu(h @ w1)) @ w2


def _trunk_layer(xc, xl, c, c_ctx, w_ada, b_ada, w_in, lb_fwd, lb_bwd, hgrn_norm, pool_w, pool_scale,
                 attn_sink, w_branch, w_out, ln1_g, ln1_b, w_ff1, w_ff2, ln2_g, ln2_b, need_ctx):
    mod_l = (jax.nn.silu(c) @ w_ada + b_ada)[:, None, :]
    mod_c = (jax.nn.silu(c_ctx) @ w_ada + b_ada)[None, None, :]
    sh1_l, sc1_l, g1_l, sh2_l, sc2_l, g2_l = jnp.split(mod_l, N_ADA, axis=-1)
    sh1_c, sc1_c, g1_c, sh2_c, sc2_c, g2_c = jnp.split(mod_c, N_ADA, axis=-1)

    ul = (xl * (1 + sc1_l) + sh1_l) @ w_in
    uc = (xc * (1 + sc1_c) + sh1_c) @ w_in
    ya_c, ya_l = _hgrn_mixer(uc, ul, lb_fwd, lb_bwd, hgrn_norm, need_ctx)
    yc_c, yc_l = _attention_mixer(uc, ul, attn_sink, need_ctx)
    yb_l = _pool_mixer(ul, pool_w, pool_scale)
    xl = _layer_norm(ALPHA * xl + g1_l * _merge_branches(ul, (ya_l, yb_l, yc_l), w_branch, w_out), ln1_g, ln1_b)
    xl = _layer_norm(ALPHA * xl + g2_l * _sq_relu_mlp(xl * (1 + sc2_l) + sh2_l, w_ff1, w_ff2), ln2_g, ln2_b)

    if need_ctx:
        yb_c = _pool_mixer(uc, pool_w, pool_scale)
        xc = _layer_norm(ALPHA * xc + g1_c * _merge_branches(uc, (ya_c, yb_c, yc_c), w_branch, w_out), ln1_g, ln1_b)
        xc = _layer_norm(ALPHA * xc + g2_c * _sq_relu_mlp(xc * (1 + sc2_c) + sh2_c, w_ff1, w_ff2), ln2_g, ln2_b)
    return xc, xl


def setup_inputs(seed: int = 0) -> dict:
    key = jax.random.key(seed)
    ks = jax.random.split(key, 20)

    def nrm(k, shape, s):
        return jax.random.normal(k, shape, jnp.float32) * s

    return {
        'x': nrm(ks[0], (BATCH, SEQ, D_MODEL), 1.0),
        'c': nrm(ks[1], (BATCH, D_MODEL), 1.0),
        'ctx': nrm(ks[2], (BATCH, CTX_LEN, D_MODEL), 1.0),
        'c_ctx': nrm(ks[3], (D_MODEL,), 1.0),
        'w_ada': nrm(ks[4], (DEPTH, D_MODEL, N_ADA * D_MODEL), 0.3 * D_MODEL ** -0.5),
        'b_ada': nrm(ks[5], (DEPTH, N_ADA * D_MODEL), 0.02),
        'w_in': nrm(ks[6], (DEPTH, D_MODEL, N_COLS), D_MODEL ** -0.5),
        'hgrn_lb': nrm(ks[7], (DEPTH, 2, WK_A), 0.1),
        'hgrn_norm': 1.0 + nrm(ks[8], (DEPTH, WA), 0.1),
        'pool_w': nrm(ks[9], (DEPTH, N_POOL, POOL_GW, POOL_GW), POOL_GW ** -0.5),
        'pool_scale': 1.0 + nrm(ks[10], (DEPTH, WB), 0.1),
        'attn_sink': nrm(ks[11], (DEPTH, HQ), 0.5),
        'w_branch': nrm(ks[12], (DEPTH, N_BRANCH, BRANCH_W, D_MODEL), BRANCH_W ** -0.5),
        'w_out': nrm(ks[13], (DEPTH, D_MODEL, D_MODEL), BETA * D_MODEL ** -0.5),
        'ln1_g': 1.0 + nrm(ks[14], (DEPTH, D_MODEL), 0.1),
        'ln1_b': nrm(ks[15], (DEPTH, D_MODEL), 0.02),
        'w_ff1': nrm(ks[16], (DEPTH, D_MODEL, D_FF), D_MODEL ** -0.5),
        'w_ff2': nrm(ks[17], (DEPTH, D_FF, D_MODEL), BETA * D_FF ** -0.5),
        'ln2_g': 1.0 + nrm(ks[18], (DEPTH, D_MODEL), 0.1),
        'ln2_b': nrm(ks[19], (DEPTH, D_MODEL), 0.02),
    }


def reference(x, c, ctx, c_ctx, w_ada, b_ada, w_in, hgrn_lb, hgrn_norm, pool_w, pool_scale, attn_sink,
              w_branch, w_out, ln1_g, ln1_b, w_ff1, w_ff2, ln2_g, ln2_b):
    lb = jnp.cumsum(jax.nn.softmax(hgrn_lb.astype(jnp.float32), axis=0), axis=0)
    lb = lb - lb[0:1]
    xc, xl = ctx, x
    for l in range(DEPTH):
        xc, xl = _trunk_layer(xc, xl, c, c_ctx, w_ada[l], b_ada[l], w_in[l], lb[l, 0], lb[l, 1], hgrn_norm[l],
                              pool_w[l], pool_scale[l], attn_sink[l], w_branch[l], w_out[l], ln1_g[l], ln1_b[l],
                              w_ff1[l], w_ff2[l], ln2_g[l], ln2_b[l], l < DEPTH - 1)
    return xl
```

```python
import functools

import jax
import jax.numpy as jnp
from jax import lax
from jax.experimental import pallas as pl
from jax.experimental.pallas import tpu as pltpu

F32 = jnp.float32
BF16 = jnp.bfloat16

D_MODEL = 2048
GRID_W = 64
HA, DK_A = 8, 128
WK_A = HA * DK_A
POOL_WINDOWS = (2, 4, 8, 16)
POOL_GW = 256
HQ, HKV, HD = 8, 2, 128
GQ = HQ // HKV
WINDOW = 128
BLOCK_Q = 128
ROPE_THETA = 10000.0
N_BRANCH = 3
BRANCH_W = 1024
N_ADA = 6
LN_EPS = 1e-5
RMS_EPS = 1e-6

COL_AQ = 0
COL_AI = COL_AQ + WK_A
COL_AG = COL_AI + BRANCH_W
COL_AFF = COL_AG + BRANCH_W
COL_AFB = COL_AFF + WK_A
COL_B = COL_AFB + WK_A
COL_CQ = COL_B + BRANCH_W
COL_CK = COL_CQ + BRANCH_W
COL_CV = COL_CK + HKV * HD
COL_GATE = COL_CV + HKV * HD
N_COLS = COL_GATE + N_BRANCH * D_MODEL

LANES = 128
VMEM_LIMIT = 56 * 1024 * 1024

CHUNK = 64
SUB = 16
MOD_ROWS = 16


def _cparams(n_axes):
    return pltpu.CompilerParams(dimension_semantics=("arbitrary",) * n_axes, vmem_limit_bytes=VMEM_LIMIT)


def _dot(a, b):
    return jnp.dot(a, b, preferred_element_type=F32)


def _dot_nt(a, b):
    return lax.dot_general(a, b, (((1,), (1,)), ((), ())), preferred_element_type=F32)


def _dot_tn(a, b):
    return lax.dot_general(a, b, (((0,), (0,)), ((), ())), preferred_element_type=F32)


def _mod_kernel(c_ref, w_ref, b_ref, o_ref):
    c = c_ref[...]
    s = (c * jax.nn.sigmoid(c)).astype(BF16)
    o_ref[...] = _dot(s, w_ref[...].astype(BF16)) + b_ref[...]


def _modulation(cc, w_ada, b_ada):
    depth, d, n = w_ada.shape
    tn = 1024
    return pl.pallas_call(
        _mod_kernel,
        grid=(depth, n // tn),
        in_specs=[
            pl.BlockSpec((MOD_ROWS, d), lambda l, j: (0, 0)),
            pl.BlockSpec((None, d, tn), lambda l, j: (l, 0, j)),
            pl.BlockSpec((None, 1, tn), lambda l, j: (l, 0, j)),
        ],
        out_specs=pl.BlockSpec((None, MOD_ROWS, tn), lambda l, j: (l, 0, j)),
        out_shape=jax.ShapeDtypeStruct((depth, MOD_ROWS, n), F32),
        compiler_params=_cparams(2),
        name="adaln_mod",
    )(cc, w_ada, b_ada.reshape(depth, 1, n))


def _modmm_kernel(x_ref, mod_ref, w_ref, o_ref, xm_ref, *, sh_idx, sc_idx, sq_relu):
    @pl.when(pl.program_id(2) == 0)
    def _():
        sh = mod_ref[sh_idx:sh_idx + 1, :]
        sc = mod_ref[sc_idx:sc_idx + 1, :]
        xm_ref[...] = (x_ref[...] * (1.0 + sc) + sh).astype(BF16)

    y = _dot(xm_ref[...], w_ref[...])
    if sq_relu:
        y = jnp.square(jnp.maximum(y, 0.0))
    o_ref[...] = y.astype(o_ref.dtype)


def _mod_matmul(x, mod, w, *, ctx_row, sh_idx, sc_idx, sq_relu, out_dtype, tm, tn, name):
    b, t, d = x.shape
    n = w.shape[1]
    row = (lambda bi: MOD_ROWS // 2) if ctx_row else (lambda bi: bi)
    return pl.pallas_call(
        functools.partial(_modmm_kernel, sh_idx=sh_idx, sc_idx=sc_idx, sq_relu=sq_relu),
        grid=(b, t // tm, n // tn),
        in_specs=[
            pl.BlockSpec((None, tm, d), lambda bi, i, j: (bi, i, 0)),
            pl.BlockSpec((None, N_ADA, d), lambda bi, i, j: (row(bi), 0, 0)),
            pl.BlockSpec((d, tn), lambda bi, i, j: (0, j)),
        ],
        out_specs=pl.BlockSpec((None, tm, tn), lambda bi, i, j: (bi, i, j)),
        out_shape=jax.ShapeDtypeStruct((b, t, n), out_dtype),
        scratch_shapes=[pltpu.VMEM((tm, d), BF16)],
        compiler_params=_cparams(3),
        name=name,
    )(x, mod, w)


def _rope_kernel(q_ref, kv_ref, cos_ref, sin_ref, qo_ref, kvo_ref):
    cos = cos_ref[...]
    sin = sin_ref[...]

    def rope(v):
        return v * cos + pltpu.roll(v, HD // 2, 1) * sin

    for h in range(HQ):
        sl = slice(h * HD, (h + 1) * HD)
        qo_ref[:, sl] = rope(q_ref[:, sl]).astype(BF16)
    for h in range(HKV):
        sl = slice(h * HD, (h + 1) * HD)
        kvo_ref[:, sl] = rope(kv_ref[:, sl]).astype(BF16)
    kvo_ref[:, HKV * HD:] = kv_ref[:, HKV * HD:].astype(BF16)


def _rope_qkv(u, cos2, sin2):
    b, t, _ = u.shape
    tq = min(512, t)
    wq, wkv = HQ * HD, 2 * HKV * HD
    return pl.pallas_call(
        _rope_kernel,
        grid=(b, t // tq),
        in_specs=[
            pl.BlockSpec((None, tq, wq), lambda bi, i: (bi, i, COL_CQ // wq)),
            pl.BlockSpec((None, tq, wkv), lambda bi, i: (bi, i, COL_CK // wkv)),
            pl.BlockSpec((tq, HD), lambda bi, i: (i, 0)),
            pl.BlockSpec((tq, HD), lambda bi, i: (i, 0)),
        ],
        out_specs=[
            pl.BlockSpec((None, tq, wq), lambda bi, i: (bi, i, 0)),
            pl.BlockSpec((None, tq, wkv), lambda bi, i: (bi, i, 0)),
        ],
        out_shape=[jax.ShapeDtypeStruct((b, t, wq), BF16), jax.ShapeDtypeStruct((b, t, wkv), BF16)],
        compiler_params=_cparams(2),
        name="rope_qkv",
    )(u, u, cos2, sin2)


def _softmax_pv(s, sink_col, v):
    m = jnp.maximum(jnp.max(s, axis=-1, keepdims=True), sink_col)
    e = jnp.exp(s - m)
    denom = jnp.sum(e, axis=-1, keepdims=True) + jnp.exp(sink_col - m)
    return _dot((e / denom).astype(BF16), v)


def _sink_column(sink_ref, kv, rows_per_head):
    r = lax.broadcasted_iota(jnp.int32, (GQ * rows_per_head, 1), 0)
    col = jnp.zeros((GQ * rows_per_head, 1), F32)
    for g in range(GQ):
        col = jnp.where(r // rows_per_head == g, sink_ref[kv * GQ + g], col)
    return col


def _attn_latent_kernel(sink_ref, q_ref, kl_ref, km_ref, kr_ref, vl_ref, vm_ref, vr_ref, kc_ref, vc_ref, o_ref,
                        *, t_len):
    kv = pl.program_id(1)
    n = pl.program_id(2)
    bq = BLOCK_Q
    q = jnp.concatenate([q_ref[:, g * HD:(g + 1) * HD] for g in range(GQ)], axis=0)
    keys = jnp.concatenate([kl_ref[...], km_ref[...], kr_ref[...], kc_ref[...].astype(BF16)], axis=0)
    vals = jnp.concatenate([vl_ref[...], vm_ref[...], vr_ref[...], vc_ref[...].astype(BF16)], axis=0)
    nk = keys.shape[0]
    s = _dot_nt(q, keys) * (HD ** -0.5)
    r = lax.broadcasted_iota(jnp.int32, (GQ * bq, nk), 0)
    c = lax.broadcasted_iota(jnp.int32, (GQ * bq, nk), 1)
    q_pos = n * bq + r % bq
    k_pos = (n - 1) * bq + c
    in_band = (jnp.abs(q_pos - k_pos) <= WINDOW) & (k_pos >= 0) & (k_pos < t_len)
    s = jnp.where(in_band | (c >= 3 * bq), s, -jnp.inf)
    o = _softmax_pv(s, _sink_column(sink_ref, kv, bq), vals)
    for g in range(GQ):
        o_ref[:, g * HD:(g + 1) * HD] = o[g * bq:(g + 1) * bq].astype(o_ref.dtype)


def _attention_latent(sink, q_r, kv_r, uc):
    b, t, _ = q_r.shape
    tc = uc.shape[1]
    nb = t // BLOCK_Q
    wq = GQ * HD
    kblk = lambda off: pl.BlockSpec(
        (None, BLOCK_Q, HD), lambda bi, kv, n: (bi, jnp.clip(n + off, 0, nb - 1), kv))
    vblk = lambda off: pl.BlockSpec(
        (None, BLOCK_Q, HD), lambda bi, kv, n: (bi, jnp.clip(n + off, 0, nb - 1), HKV + kv))
    return pl.pallas_call(
        functools.partial(_attn_latent_kernel, t_len=t),
        grid=(b, HKV, nb),
        in_specs=[
            pl.BlockSpec(memory_space=pltpu.SMEM),
            pl.BlockSpec((None, BLOCK_Q, wq), lambda bi, kv, n: (bi, n, kv)),
            kblk(-1), kblk(0), kblk(1), vblk(-1), vblk(0), vblk(1),
            pl.BlockSpec((None, tc, HD), lambda bi, kv, n: (bi, 0, COL_CK // HD + kv)),
            pl.BlockSpec((None, tc, HD), lambda bi, kv, n: (bi, 0, COL_CV // HD + kv)),
        ],
        out_specs=pl.BlockSpec((None, BLOCK_Q, wq), lambda bi, kv, n: (bi, n, kv)),
        out_shape=jax.ShapeDtypeStruct((b, t, HQ * HD), BF16),
        compiler_params=_cparams(3),
        name="attn_latent",
    )(sink, q_r, kv_r, kv_r, kv_r, kv_r, kv_r, kv_r, uc, uc)


def _attn_ctx_kernel(sink_ref, q_ref, kc_ref, vc_ref, o_ref):
    kv = pl.program_id(1)
    tc = q_ref.shape[0]
    q = jnp.concatenate([q_ref[:, g * HD:(g + 1) * HD] for g in range(GQ)], axis=0).astype(BF16)
    s = _dot_nt(q, kc_ref[...].astype(BF16)) * (HD ** -0.5)
    o = _softmax_pv(s, _sink_column(sink_ref, kv, tc), vc_ref[...].astype(BF16))
    for g in range(GQ):
        o_ref[:, g * HD:(g + 1) * HD] = o[g * tc:(g + 1) * tc].astype(o_ref.dtype)


def _attention_ctx(sink, uc):
    b, tc, _ = uc.shape
    wq = GQ * HD
    return pl.pallas_call(
        _attn_ctx_kernel,
        grid=(b, HKV),
        in_specs=[
            pl.BlockSpec(memory_space=pltpu.SMEM),
            pl.BlockSpec((None, tc, wq), lambda bi, kv: (bi, 0, COL_CQ // wq + kv)),
            pl.BlockSpec((None, tc, HD), lambda bi, kv: (bi, 0, COL_CK // HD + kv)),
            pl.BlockSpec((None, tc, HD), lambda bi, kv: (bi, 0, COL_CV // HD + kv)),
        ],
        out_specs=pl.BlockSpec((None, tc, wq), lambda bi, kv: (bi, 0, kv)),
        out_shape=jax.ShapeDtypeStruct((b, tc, HQ * HD), BF16),
        compiler_params=_cparams(2),
        name="attn_ctx",
    )(sink, uc, uc, uc)


def _pool_kernel(u_ref, w_ref, sc_ref, o_ref):
    t_len = u_ref.shape[0]
    grp = pl.program_id(1)
    x = u_ref[...]
    t = lax.broadcasted_iota(jnp.int32, x.shape, 0)

    def shift_down(a, k):
        return jnp.where(t >= k, pltpu.roll(a, k, 0), 0.0)

    def shift_up(a, k):
        return jnp.where(t < t_len - k, pltpu.roll(a, t_len - k, 0), 0.0)

    for j, win in enumerate(POOL_WINDOWS):
        @pl.when(grp == j)
        def _(win=win):
            half = win // 2
            trail, lead, step = x, x, 1
            while step < half:
                trail = trail + shift_down(trail, step)
                lead = lead + shift_up(lead, step)
                step *= 2
            total = shift_down(trail, 1) + lead
            cnt = (jnp.minimum(t + half, t_len) - jnp.maximum(t - half, 0)).astype(F32)
            y = _dot((total / cnt - x).astype(BF16), w_ref[...])
            o_ref[...] = (y * sc_ref[...]).astype(o_ref.dtype)


def _pool_mixer(u, pool_w, pool_scale):
    b, t, _ = u.shape
    ng = len(POOL_WINDOWS)
    return pl.pallas_call(
        _pool_kernel,
        grid=(b, ng),
        in_specs=[
            pl.BlockSpec((None, t, POOL_GW), lambda bi, g: (bi, 0, COL_B // POOL_GW + g)),
            pl.BlockSpec((None, POOL_GW, POOL_GW), lambda bi, g: (g, 0, 0)),
            pl.BlockSpec((1, POOL_GW), lambda bi, g: (0, g)),
        ],
        out_specs=pl.BlockSpec((None, t, POOL_GW), lambda bi, g: (bi, 0, g)),
        out_shape=jax.ShapeDtypeStruct((b, t, ng * POOL_GW), BF16),
        compiler_params=_cparams(2),
        name="pool_mixer",
    )(u, pool_w, pool_scale)


def _hgrn_kernel(*refs, rev, final, n_ctx_blocks, tb):
    if final:
        (qc_ref, ic_ref, fc_ref, ql_ref, il_ref, fl_ref, gconst_ref,
         gc_ref, gl_ref, obc_ref, obl_ref, ng_ref, oc_ref, ol_ref, st_ref) = refs
    else:
        (qc_ref, ic_ref, fc_ref, ql_ref, il_ref, fl_ref, gconst_ref, oc_ref, ol_ref, st_ref) = refs
    step = pl.program_id(2)
    is_ctx = step < n_ctx_blocks

    @pl.when(step == 0)
    def _():
        st_ref[...] = jnp.zeros_like(st_ref)

    def pick(c_ref, l_ref):
        return jnp.where(is_ctx, c_ref[...], l_ref[...])

    q = pick(qc_ref, ql_ref)
    v = pick(ic_ref, il_ref)
    z = pick(fc_ref, fl_ref)
    log_lb = gconst_ref[0:1, :]
    log_1m_lb = gconst_ref[1:2, :]
    one_m_lb = gconst_ref[2:3, :]

    e = jnp.exp(-jnp.abs(z))
    r = 1.0 / (1.0 + e)
    log_sig = jnp.minimum(z, 0.0) - jnp.log1p(e)
    bb = log_1m_lb + log_sig
    logf = jnp.maximum(log_lb, bb) + jnp.log1p(jnp.exp(-jnp.abs(log_lb - bb)))
    k = one_m_lb * jnp.where(z >= 0.0, e * r, r)

    ri = lax.broadcasted_iota(jnp.int32, (tb, tb), 0)
    ci = lax.broadcasted_iota(jnp.int32, (tb, tb), 1)
    ordered = (ci >= ri) if rev else (ci <= ri)
    cum_mat = jnp.where(ri // CHUNK == ci // CHUNK, jnp.where(ordered, 1.0, 0.0), 0.0).astype(BF16)
    hi = logf.astype(BF16)
    rem = logf - hi.astype(F32)
    mid = rem.astype(BF16)
    lo = (rem - mid.astype(F32)).astype(BF16)
    b = _dot(cum_mat, hi) + _dot(cum_mat, mid) + _dot(cum_mat, lo)

    pos = lax.broadcasted_iota(jnp.int32, (tb, LANES), 0) % SUB
    o_diag = jnp.zeros((tb, LANES), F32)
    for d in range(SUB):
        if d == 0:
            kd, bd, vd = k, b, v
        else:
            sh = tb - d if rev else d
            kd, bd, vd = pltpu.roll(k, sh, 0), pltpu.roll(b, sh, 0), pltpu.roll(v, sh, 0)
        a = jnp.sum(q * kd * jnp.exp(jnp.minimum(b - bd, 0.0)), axis=-1, keepdims=True)
        valid = (pos <= SUB - 1 - d) if rev else (pos >= d)
        o_diag = o_diag + jnp.where(valid, a, 0.0) * vd

    n_sub = CHUNK // SUB
    st = st_ref[...]
    chunk_out = [None] * (tb // CHUNK)
    chunk_ids = range(tb // CHUNK)
    for cidx in (reversed(chunk_ids) if rev else chunk_ids):
        r0 = cidx * CHUNK
        qc, kc, vc, bc = (a[r0:r0 + CHUNK] for a in (q, k, v, b))
        b_end = bc[0:1] if rev else bc[CHUNK - 1:CHUNK]
        o_inter = _dot_nt((qc * jnp.exp(bc)).astype(BF16), st.astype(BF16))
        parts = []
        for i in range(n_sub):
            rows = slice(i * SUB, (i + 1) * SUB)
            if rev:
                src = slice((i + 1) * SUB, CHUNK)
                b_ref = bc[(i + 1) * SUB:(i + 1) * SUB + 1] if i + 1 < n_sub else None
            else:
                src = slice(0, i * SUB)
                b_ref = bc[i * SUB - 1:i * SUB] if i > 0 else None
            if b_ref is None:
                parts.append(jnp.zeros((SUB, LANES), F32))
                continue
            qi = (qc[rows] * jnp.exp(bc[rows] - b_ref)).astype(BF16)
            ks = (kc[src] * jnp.exp(b_ref - bc[src])).astype(BF16)
            parts.append(_dot(_dot_nt(qi, ks).astype(BF16), vc[src].astype(BF16)))
        chunk_out[cidx] = o_inter + jnp.concatenate(parts, axis=0)
        kt = (kc * jnp.exp(b_end - bc)).astype(BF16)
        st = st * jnp.exp(b_end) + _dot_tn(vc.astype(BF16), kt)
    st_ref[...] = st
    o = jnp.concatenate(chunk_out, axis=0) + o_diag

    if final:
        o = o + pick(obc_ref, obl_ref)
        g = pick(gc_ref, gl_ref)
        o = o * lax.rsqrt(jnp.mean(jnp.square(o), axis=-1, keepdims=True) + RMS_EPS)
        o = o * ng_ref[...] * (g * jax.nn.sigmoid(g))

    @pl.when(is_ctx)
    def _():
        oc_ref[...] = o.astype(oc_ref.dtype)

    @pl.when(jnp.logical_not(is_ctx))
    def _():
        ol_ref[...] = o.astype(ol_ref.dtype)


def _hgrn_direction(uc, ul, gconst, rev, final_inputs=None):
    b, tc, _ = uc.shape
    t = ul.shape[1]
    tb = min(256, tc)
    ncb, nlb = tc // tb, t // tb
    final = final_inputs is not None

    if rev:
        c_row = lambda s: jnp.maximum(ncb - 1 - s, 0)
        l_row = lambda s: nlb - 1 - jnp.maximum(s - ncb, 0)
    else:
        c_row = lambda s: jnp.minimum(s, ncb - 1)
        l_row = lambda s: jnp.maximum(s - ncb, 0)

    def col_spec(row_fn, col0):
        return pl.BlockSpec((None, tb, LANES), lambda bi, h, s: (bi, row_fn(s), col0 // LANES + h))

    col_f = COL_AFB if rev else COL_AFF
    in_specs = [col_spec(c_row, COL_AQ), col_spec(c_row, COL_AI), col_spec(c_row, col_f),
                col_spec(l_row, COL_AQ), col_spec(l_row, COL_AI), col_spec(l_row, col_f),
                pl.BlockSpec((3, LANES), lambda bi, h, s: (0, h))]
    args = [uc, uc, uc, ul, ul, ul, gconst]
    if final:
        ob_c, ob_l, norm_g = final_inputs
        in_specs += [col_spec(c_row, COL_AG), col_spec(l_row, COL_AG), col_spec(c_row, 0), col_spec(l_row, 0),
                     pl.BlockSpec((1, LANES), lambda bi, h, s: (0, h))]
        args += [uc, ul, ob_c, ob_l, norm_g]
    out_dtype = BF16 if final else F32
    return pl.pallas_call(
        functools.partial(_hgrn_kernel, rev=rev, final=final, n_ctx_blocks=ncb, tb=tb),
        grid=(b, HA, ncb + nlb),
        in_specs=in_specs,
        out_specs=[col_spec(c_row, 0), col_spec(l_row, 0)],
        out_shape=[jax.ShapeDtypeStruct((b, tc, BRANCH_W), out_dtype),
                   jax.ShapeDtypeStruct((b, t, BRANCH_W), out_dtype)],
        scratch_shapes=[pltpu.VMEM((LANES, LANES), F32)],
        compiler_params=_cparams(3),
        name="hgrn_fwd" if final else "hgrn_bwd",
    )(*args)


def _merge_kernel(ya_ref, yb_ref, yc_ref, wb_ref, g0_ref, g1_ref, g2_ref, o_ref):
    acc = jax.nn.sigmoid(g0_ref[...]) * _dot(ya_ref[...], wb_ref[0])
    acc = acc + jax.nn.sigmoid(g1_ref[...]) * _dot(yb_ref[...], wb_ref[1])
    acc = acc + jax.nn.sigmoid(g2_ref[...]) * _dot(yc_ref[...], wb_ref[2])
    o_ref[...] = acc.astype(o_ref.dtype)


def _merge_branches(u, ya, yb, yc, w_branch, tm):
    b, t, _ = u.shape
    d = w_branch.shape[2]
    tn = 512
    y_spec = pl.BlockSpec((None, tm, BRANCH_W), lambda bi, i, j: (bi, i, 0))
    gate = lambda br: pl.BlockSpec(
        (None, tm, tn), lambda bi, i, j: (bi, i, (COL_GATE + br * d) // tn + j))
    return pl.pallas_call(
        _merge_kernel,
        grid=(b, t // tm, d // tn),
        in_specs=[y_spec, y_spec, y_spec,
                  pl.BlockSpec((N_BRANCH, BRANCH_W, tn), lambda bi, i, j: (0, 0, j)),
                  gate(0), gate(1), gate(2)],
        out_specs=pl.BlockSpec((None, tm, tn), lambda bi, i, j: (bi, i, j)),
        out_shape=jax.ShapeDtypeStruct((b, t, d), BF16),
        compiler_params=_cparams(3),
        name="merge_branches",
    )(ya, yb, yc, w_branch, u, u, u)


def _mmln_kernel(a_ref, w_ref, x_ref, mod_ref, g_ref, b_ref, o_ref, acc_ref, *, gate_idx, nk, alpha):
    kk = pl.program_id(2)
    p = _dot(a_ref[...], w_ref[...])

    def finish(y):
        res = alpha * x_ref[...] + mod_ref[gate_idx:gate_idx + 1, :] * y
        mu = jnp.mean(res, axis=-1, keepdims=True)
        cen = res - mu
        var = jnp.mean(jnp.square(cen), axis=-1, keepdims=True)
        o_ref[...] = cen * lax.rsqrt(var + LN_EPS) * g_ref[...] + b_ref[...]

    if nk == 1:
        finish(p)
    else:
        @pl.when(kk == 0)
        def _():
            acc_ref[...] = p

        @pl.when(jnp.logical_and(kk > 0, kk < nk - 1))
        def _():
            acc_ref[...] += p

        @pl.when(kk == nk - 1)
        def _():
            finish(acc_ref[...] + p)


def _matmul_res_ln(a, w, x, mod, ln_g, ln_b, *, ctx_row, gate_idx, alpha, tm, tk, name):
    b, t, kdim = a.shape
    d = w.shape[1]
    nk = kdim // tk
    row = (lambda bi: MOD_ROWS // 2) if ctx_row else (lambda bi: bi)
    return pl.pallas_call(
        functools.partial(_mmln_kernel, gate_idx=gate_idx, nk=nk, alpha=alpha),
        grid=(b, t // tm, nk),
        in_specs=[
            pl.BlockSpec((None, tm, tk), lambda bi, i, kk: (bi, i, kk)),
            pl.BlockSpec((tk, d), lambda bi, i, kk: (kk, 0)),
            pl.BlockSpec((None, tm, d), lambda bi, i, kk: (bi, i, 0)),
            pl.BlockSpec((None, N_ADA, d), lambda bi, i, kk: (row(bi), 0, 0)),
            pl.BlockSpec((1, d), lambda bi, i, kk: (0, 0)),
            pl.BlockSpec((1, d), lambda bi, i, kk: (0, 0)),
        ],
        out_specs=pl.BlockSpec((None, tm, d), lambda bi, i, kk: (bi, i, 0)),
        out_shape=jax.ShapeDtypeStruct((b, t, d), F32),
        scratch_shapes=[pltpu.VMEM((tm, d), F32)],
        compiler_params=_cparams(3),
        name=name,
    )(a, w, x, mod, ln_g, ln_b)


def _rope_tables(t_len):
    rows = t_len // GRID_W
    row = jnp.repeat(jnp.arange(rows, dtype=F32), GRID_W)
    col = jnp.tile(jnp.arange(GRID_W, dtype=F32), rows)
    n_freq = HD // 4
    inv_freq = ROPE_THETA ** (-jnp.arange(n_freq, dtype=F32) / n_freq)
    ang = jnp.concatenate([row[:, None] * inv_freq, col[:, None] * inv_freq], -1)
    cos, sin = jnp.cos(ang), jnp.sin(ang)
    return jnp.concatenate([cos, cos], -1), jnp.concatenate([-sin, sin], -1)


def kernel(x, c, ctx, c_ctx, w_ada, b_ada, w_in, hgrn_lb, hgrn_norm, pool_w, pool_scale, attn_sink, w_branch,
           w_out, ln1_g, ln1_b, w_ff1, w_ff2, ln2_g, ln2_b):
    depth = w_ada.shape[0]
    bsz, t_len, d = x.shape
    tc_len = ctx.shape[1]
    assert d == D_MODEL and w_in.shape[2] == N_COLS and bsz <= MOD_ROWS // 2
    alpha = (2.0 * depth) ** 0.25

    lb = jnp.cumsum(jax.nn.softmax(hgrn_lb.astype(F32), axis=0), axis=0)
    lb = lb - lb[0:1]
    gconst = jnp.stack([jnp.log(lb), jnp.log1p(-lb), 1.0 - lb], axis=2)

    cc = jnp.zeros((MOD_ROWS, d), F32).at[:bsz].set(c).at[MOD_ROWS // 2].set(c_ctx)
    mod = _modulation(cc, w_ada, b_ada).reshape(depth, MOD_ROWS, N_ADA, d)
    cos2, sin2 = _rope_tables(t_len)

    tm_l = min(1024, t_len)
    tm_c = min(1024, tc_len)
    tm_ff_l = min(512, t_len)
    tm_ff_c = min(512, tc_len)
    d_ff = w_ff1.shape[2]

    xc, xl = ctx, x
    for l in range(depth):
        need_ctx = l < depth - 1
        w_in_l = w_in[l].astype(BF16)
        w_br_l = w_branch[l].astype(BF16)
        w_out_l = w_out[l].astype(BF16)
        w_ff1_l = w_ff1[l].astype(BF16)
        w_ff2_l = w_ff2[l].astype(BF16)
        pool_w_l = pool_w[l].astype(BF16)
        mod_l = mod[l]
        g1, b1 = ln1_g[l].reshape(1, d), ln1_b[l].reshape(1, d)
        g2, b2 = ln2_g[l].reshape(1, d), ln2_b[l].reshape(1, d)
        sink_l = attn_sink[l].astype(F32)
        scale_l = pool_scale[l].reshape(1, -1)
        norm_l = hgrn_norm[l].reshape(1, -1)

        in_proj = functools.partial(_mod_matmul, sh_idx=0, sc_idx=1, sq_relu=False, out_dtype=F32, tn=768)
        ul = in_proj(xl, mod_l, w_in_l, ctx_row=False, tm=tm_l, name="in_proj_latent")
        uc = in_proj(xc, mod_l, w_in_l, ctx_row=True, tm=tm_c, name="in_proj_ctx")

        ob_c, ob_l = _hgrn_direction(uc, ul, gconst[l, 1], rev=True)
        ya_c, ya_l = _hgrn_direction(uc, ul, gconst[l, 0], rev=False, final_inputs=(ob_c, ob_l, norm_l))
        yb_l = _pool_mixer(ul, pool_w_l, scale_l)
        q_r, kv_r = _rope_qkv(ul, cos2, sin2)
        yc_l = _attention_latent(sink_l, q_r, kv_r, uc)

        def finish_stream(xs, us, ya, yb, yc, ctx_row, tm, tm_ff):
            merged = _merge_branches(us, ya, yb, yc, w_br_l, tm)
            xs = _matmul_res_ln(merged, w_out_l, xs, mod_l, g1, b1, ctx_row=ctx_row, gate_idx=2, alpha=alpha,
                                tm=tm_ff, tk=d, name="out_proj_ln")
            hidden = _mod_matmul(xs, mod_l, w_ff1_l, ctx_row=ctx_row, sh_idx=3, sc_idx=4, sq_relu=True,
                                 out_dtype=BF16, tm=tm, tn=1024, name="ffn_up")
            return _matmul_res_ln(hidden, w_ff2_l, xs, mod_l, g2, b2, ctx_row=ctx_row, gate_idx=5, alpha=alpha,
                                  tm=tm_ff, tk=min(1024, d_ff), name="ffn_down_ln")

        xl = finish_stream(xl, ul, ya_l, yb_l, yc_l, False, tm_l, tm_ff_l)
        if need_ctx:
            yb_c = _pool_mixer(uc, pool_w_l, scale_l)
            yc_c = _attention_ctx(sink_l, uc)
            xc = finish_stream(xc, uc, ya_c, yb_c, yc_c, True, tm_c, tm_ff_c)
    return xl
```

```python
import functools

import jax
import jax.numpy as jnp
from jax import lax
from jax.experimental import pallas as pl
from jax.experimental.pallas import tpu as pltpu

F32 = jnp.float32
BF16 = jnp.bfloat16

D_MODEL = 2048
GRID_W = 64
HA, DK_A = 8, 128
WK_A = HA * DK_A
POOL_WINDOWS = (2, 4, 8, 16)
POOL_GW = 256
HQ, HKV, HD = 8, 2, 128
GQ = HQ // HKV
WINDOW = 128
BLOCK_Q = 128
ROPE_THETA = 10000.0
N_BRANCH = 3
BRANCH_W = 1024
N_ADA = 6
LN_EPS = 1e-5
RMS_EPS = 1e-6

COL_AQ = 0
COL_AI = COL_AQ + WK_A
COL_AG = COL_AI + BRANCH_W
COL_AFF = COL_AG + BRANCH_W
COL_AFB = COL_AFF + WK_A
COL_B = COL_AFB + WK_A
COL_CQ = COL_B + BRANCH_W
COL_CK = COL_CQ + BRANCH_W
COL_CV = COL_CK + HKV * HD
COL_GATE = COL_CV + HKV * HD
N_COLS = COL_GATE + N_BRANCH * D_MODEL

LANES = 128
VMEM_LIMIT = 56 * 1024 * 1024

CHUNK = 64
SUB = 16
HGRN_HEADS_PER_STEP = 2
MOD_ROWS = 16


def _cparams(n_axes):
    return pltpu.CompilerParams(dimension_semantics=("arbitrary",) * n_axes, vmem_limit_bytes=VMEM_LIMIT)


def _dot(a, b):
    return jnp.dot(a, b, preferred_element_type=F32)


def _dot_nt(a, b):
    return lax.dot_general(a, b, (((1,), (1,)), ((), ())), preferred_element_type=F32)


def _dot_tn(a, b):
    return lax.dot_general(a, b, (((0,), (0,)), ((), ())), preferred_element_type=F32)


def _mod_kernel(c_ref, w_ref, b_ref, o_ref):
    c = c_ref[...]
    s = (c * jax.nn.sigmoid(c)).astype(BF16)
    o_ref[...] = _dot(s, w_ref[...].astype(BF16)) + b_ref[...]


def _modulation(cc, w_ada, b_ada):
    depth, d, n = w_ada.shape
    tn = 1024
    return pl.pallas_call(
        _mod_kernel,
        grid=(depth, n // tn),
        in_specs=[
            pl.BlockSpec((MOD_ROWS, d), lambda l, j: (0, 0)),
            pl.BlockSpec((None, d, tn), lambda l, j: (l, 0, j)),
            pl.BlockSpec((None, 1, tn), lambda l, j: (l, 0, j)),
        ],
        out_specs=pl.BlockSpec((None, MOD_ROWS, tn), lambda l, j: (l, 0, j)),
        out_shape=jax.ShapeDtypeStruct((depth, MOD_ROWS, n), F32),
        compiler_params=_cparams(2),
        name="adaln_mod",
    )(cc, w_ada, b_ada.reshape(depth, 1, n))


def _modmm_kernel(x_ref, mod_ref, w_ref, o_ref, xm_ref, *, sh_idx, sc_idx, sq_relu):
    @pl.when(pl.program_id(2) == 0)
    def _():
        sh = mod_ref[sh_idx:sh_idx + 1, :]
        sc = mod_ref[sc_idx:sc_idx + 1, :]
        xm_ref[...] = (x_ref[...] * (1.0 + sc) + sh).astype(BF16)

    y = _dot(xm_ref[...], w_ref[...])
    if sq_relu:
        y = jnp.square(jnp.maximum(y, 0.0))
    o_ref[...] = y.astype(o_ref.dtype)


def _mod_matmul(x, mod, w, layer, *, ctx_row, sh_idx, sc_idx, sq_relu, out_dtype, tm, tn, name):
    b, t, d = x.shape
    n = w.shape[2]
    row = (lambda bi: MOD_ROWS // 2) if ctx_row else (lambda bi: bi)
    return pl.pallas_call(
        functools.partial(_modmm_kernel, sh_idx=sh_idx, sc_idx=sc_idx, sq_relu=sq_relu),
        grid=(b, t // tm, n // tn),
        in_specs=[
            pl.BlockSpec((None, tm, d), lambda bi, i, j: (bi, i, 0)),
            pl.BlockSpec((None, N_ADA, d), lambda bi, i, j: (row(bi), 0, 0)),
            pl.BlockSpec((None, d, tn), lambda bi, i, j: (layer, 0, j)),
        ],
        out_specs=pl.BlockSpec((None, tm, tn), lambda bi, i, j: (bi, i, j)),
        out_shape=jax.ShapeDtypeStruct((b, t, n), out_dtype),
        scratch_shapes=[pltpu.VMEM((tm, d), BF16)],
        compiler_params=_cparams(3),
        name=name,
    )(x, mod, w)


def _rope_kernel(q_ref, kv_ref, cos_ref, sin_ref, qo_ref, kvo_ref):
    cos = cos_ref[...]
    sin = sin_ref[...]

    def rope(v):
        return v * cos + pltpu.roll(v, HD // 2, 1) * sin

    for h in range(HQ):
        sl = slice(h * HD, (h + 1) * HD)
        qo_ref[:, sl] = rope(q_ref[:, sl]).astype(BF16)
    for h in range(HKV):
        sl = slice(h * HD, (h + 1) * HD)
        kvo_ref[:, sl] = rope(kv_ref[:, sl]).astype(BF16)
    kvo_ref[:, HKV * HD:] = kv_ref[:, HKV * HD:].astype(BF16)


def _rope_qkv(u, cos2, sin2):
    b, t, _ = u.shape
    tq = min(512, t)
    wq, wkv = HQ * HD, 2 * HKV * HD
    return pl.pallas_call(
        _rope_kernel,
        grid=(b, t // tq),
        in_specs=[
            pl.BlockSpec((None, tq, wq), lambda bi, i: (bi, i, COL_CQ // wq)),
            pl.BlockSpec((None, tq, wkv), lambda bi, i: (bi, i, COL_CK // wkv)),
            pl.BlockSpec((tq, HD), lambda bi, i: (i, 0)),
            pl.BlockSpec((tq, HD), lambda bi, i: (i, 0)),
        ],
        out_specs=[
            pl.BlockSpec((None, tq, wq), lambda bi, i: (bi, i, 0)),
            pl.BlockSpec((None, tq, wkv), lambda bi, i: (bi, i, 0)),
        ],
        out_shape=[jax.ShapeDtypeStruct((b, t, wq), BF16), jax.ShapeDtypeStruct((b, t, wkv), BF16)],
        compiler_params=_cparams(2),
        name="rope_qkv",
    )(u, u, cos2, sin2)


def _softmax_pv(s, sink_col, v):
    m = jnp.maximum(jnp.max(s, axis=-1, keepdims=True), sink_col)
    e = jnp.exp(s - m)
    denom = jnp.sum(e, axis=-1, keepdims=True) + jnp.exp(sink_col - m)
    return _dot((e / denom).astype(BF16), v)


def _sink_column(sink_ref, kv, rows_per_head):
    r = lax.broadcasted_iota(jnp.int32, (GQ * rows_per_head, 1), 0)
    col = jnp.zeros((GQ * rows_per_head, 1), F32)
    for g in range(GQ):
        col = jnp.where(r // rows_per_head == g, sink_ref[kv * GQ + g], col)
    return col


def _attn_latent_kernel(sink_ref, q_ref, kl_ref, km_ref, kr_ref, vl_ref, vm_ref, vr_ref, kc_ref, vc_ref, o_ref,
                        *, t_len):
    kv = pl.program_id(1)
    n = pl.program_id(2)
    bq = BLOCK_Q
    q = jnp.concatenate([q_ref[:, g * HD:(g + 1) * HD] for g in range(GQ)], axis=0)
    keys = jnp.concatenate([kl_ref[...], km_ref[...], kr_ref[...], kc_ref[...].astype(BF16)], axis=0)
    vals = jnp.concatenate([vl_ref[...], vm_ref[...], vr_ref[...], vc_ref[...].astype(BF16)], axis=0)
    nk = keys.shape[0]
    s = _dot_nt(q, keys) * (HD ** -0.5)
    r = lax.broadcasted_iota(jnp.int32, (GQ * bq, nk), 0)
    c = lax.broadcasted_iota(jnp.int32, (GQ * bq, nk), 1)
    q_pos = n * bq + r % bq
    k_pos = (n - 1) * bq + c
    in_band = (jnp.abs(q_pos - k_pos) <= WINDOW) & (k_pos >= 0) & (k_pos < t_len)
    s = jnp.where(in_band | (c >= 3 * bq), s, -jnp.inf)
    o = _softmax_pv(s, _sink_column(sink_ref, kv, bq), vals)
    for g in range(GQ):
        o_ref[:, g * HD:(g + 1) * HD] = o[g * bq:(g + 1) * bq].astype(o_ref.dtype)


def _attention_latent(sink, q_r, kv_r, uc):
    b, t, _ = q_r.shape
    tc = uc.shape[1]
    nb = t // BLOCK_Q
    wq = GQ * HD
    kblk = lambda off: pl.BlockSpec(
        (None, BLOCK_Q, HD), lambda bi, kv, n: (bi, jnp.clip(n + off, 0, nb - 1), kv))
    vblk = lambda off: pl.BlockSpec(
        (None, BLOCK_Q, HD), lambda bi, kv, n: (bi, jnp.clip(n + off, 0, nb - 1), HKV + kv))
    return pl.pallas_call(
        functools.partial(_attn_latent_kernel, t_len=t),
        grid=(b, HKV, nb),
        in_specs=[
            pl.BlockSpec(memory_space=pltpu.SMEM),
            pl.BlockSpec((None, BLOCK_Q, wq), lambda bi, kv, n: (bi, n, kv)),
            kblk(-1), kblk(0), kblk(1), vblk(-1), vblk(0), vblk(1),
            pl.BlockSpec((None, tc, HD), lambda bi, kv, n: (bi, 0, COL_CK // HD + kv)),
            pl.BlockSpec((None, tc, HD), lambda bi, kv, n: (bi, 0, COL_CV // HD + kv)),
        ],
        out_specs=pl.BlockSpec((None, BLOCK_Q, wq), lambda bi, kv, n: (bi, n, kv)),
        out_shape=jax.ShapeDtypeStruct((b, t, HQ * HD), BF16),
        compiler_params=_cparams(3),
        name="attn_latent",
    )(sink, q_r, kv_r, kv_r, kv_r, kv_r, kv_r, kv_r, uc, uc)


def _attn_ctx_kernel(sink_ref, q_ref, kc_ref, vc_ref, o_ref):
    kv = pl.program_id(1)
    tc = q_ref.shape[0]
    q = jnp.concatenate([q_ref[:, g * HD:(g + 1) * HD] for g in range(GQ)], axis=0).astype(BF16)
    s = _dot_nt(q, kc_ref[...].astype(BF16)) * (HD ** -0.5)
    o = _softmax_pv(s, _sink_column(sink_ref, kv, tc), vc_ref[...].astype(BF16))
    for g in range(GQ):
        o_ref[:, g * HD:(g + 1) * HD] = o[g * tc:(g + 1) * tc].astype(o_ref.dtype)


def _attention_ctx(sink, uc):
    b, tc, _ = uc.shape
    wq = GQ * HD
    return pl.pallas_call(
        _attn_ctx_kernel,
        grid=(b, HKV),
        in_specs=[
            pl.BlockSpec(memory_space=pltpu.SMEM),
            pl.BlockSpec((None, tc, wq), lambda bi, kv: (bi, 0, COL_CQ // wq + kv)),
            pl.BlockSpec((None, tc, HD), lambda bi, kv: (bi, 0, COL_CK // HD + kv)),
            pl.BlockSpec((None, tc, HD), lambda bi, kv: (bi, 0, COL_CV // HD + kv)),
        ],
        out_specs=pl.BlockSpec((None, tc, wq), lambda bi, kv: (bi, 0, kv)),
        out_shape=jax.ShapeDtypeStruct((b, tc, HQ * HD), BF16),
        compiler_params=_cparams(2),
        name="attn_ctx",
    )(sink, uc, uc, uc)


def _pool_kernel(u_ref, w_ref, sc_ref, o_ref):
    t_len = u_ref.shape[0]
    grp = pl.program_id(1)
    x = u_ref[...]
    t = lax.broadcasted_iota(jnp.int32, x.shape, 0)

    def shift_down(a, k):
        return jnp.where(t >= k, pltpu.roll(a, k, 0), 0.0)

    def shift_up(a, k):
        return jnp.where(t < t_len - k, pltpu.roll(a, t_len - k, 0), 0.0)

    for j, win in enumerate(POOL_WINDOWS):
        @pl.when(grp == j)
        def _(win=win):
            half = win // 2
            trail, lead, step = x, x, 1
            while step < half:
                trail = trail + shift_down(trail, step)
                lead = lead + shift_up(lead, step)
                step *= 2
            total = shift_down(trail, 1) + lead
            cnt = (jnp.minimum(t + half, t_len) - jnp.maximum(t - half, 0)).astype(F32)
            y = _dot((total / cnt - x).astype(BF16), w_ref[...])
            o_ref[...] = (y * sc_ref[...]).astype(o_ref.dtype)


def _pool_mixer(u, pool_w, layer, pool_scale):
    b, t, _ = u.shape
    ng = len(POOL_WINDOWS)
    return pl.pallas_call(
        _pool_kernel,
        grid=(b, ng),
        in_specs=[
            pl.BlockSpec((None, t, POOL_GW), lambda bi, g: (bi, 0, COL_B // POOL_GW + g)),
            pl.BlockSpec((None, None, POOL_GW, POOL_GW), lambda bi, g: (layer, g, 0, 0)),
            pl.BlockSpec((1, POOL_GW), lambda bi, g: (0, g)),
        ],
        out_specs=pl.BlockSpec((None, t, POOL_GW), lambda bi, g: (bi, 0, g)),
        out_shape=jax.ShapeDtypeStruct((b, t, ng * POOL_GW), BF16),
        compiler_params=_cparams(2),
        name="pool_mixer",
    )(u, pool_w, pool_scale)


SAFE_LOG_DECAY = 80.0
F32_TINY = 1.1754944e-38
DIAG_CODE, OFF_CODE = 2.0, 1.0


def _hgrn_pair_codes(tb, rev):
    r = jnp.arange(tb)[:, None]
    c = jnp.arange(tb)[None, :]
    ordered = (c >= r) if rev else (c <= r)
    same_chunk = (r // CHUNK) == (c // CHUNK)
    same_sub = (r // SUB) == (c // SUB)
    return jnp.where(same_sub & ordered, DIAG_CODE, jnp.where(same_chunk & ordered, OFF_CODE, 0.0)).astype(F32)


def _hgrn_kernel(*refs, rev, final, has_init, tb):
    refs = list(refs)
    q_ref, i_ref, f_ref, gconst_ref, code_ref = refs[:5]
    del refs[:5]
    s0_ref = refs.pop(0) if has_init else None
    if final:
        g_ref, ob_ref, ng_ref = refs[:3]
        del refs[:3]
    o_ref, s_out_ref, st_ref, k_ref, bsub_ref, rest_ref = refs
    step = pl.program_id(2)
    n_chunks = tb // CHUNK
    n_sub = CHUNK // SUB
    heads = st_ref.shape[0]

    @pl.when(step == 0)
    def _():
        st_ref[...] = s0_ref[...] if has_init else jnp.zeros_like(st_ref)

    code = code_ref[...]
    row = lax.broadcasted_iota(jnp.int32, (tb, LANES), 0)
    pos_in_sub = row % SUB
    sub_of_row = (row % CHUNK) // SUB
    chunk_of_row = row // CHUNK
    sub_order = list(reversed(range(n_sub))) if rev else list(range(n_sub))
    chunk_order = list(reversed(range(n_chunks))) if rev else list(range(n_chunks))

    def rows_of(values, height):
        return jnp.concatenate([jnp.broadcast_to(val, (height, LANES)) for val in values], axis=0)

    def diag_exact(q, k, v, b_sub):
        pos = pos_in_sub
        acc = jnp.zeros((tb, LANES), F32)
        for d in range(SUB):
            if d == 0:
                kd, bd, vd = k, b_sub, v
            else:
                sh = tb - d if rev else d
                kd, bd, vd = pltpu.roll(k, sh, 0), pltpu.roll(b_sub, sh, 0), pltpu.roll(v, sh, 0)
            a = jnp.sum(q * kd * jnp.exp(jnp.minimum(b_sub - bd, 0.0)), axis=-1, keepdims=True)
            valid = (pos <= SUB - 1 - d) if rev else (pos >= d)
            acc = acc + jnp.where(valid, a, 0.0) * vd
        return acc

    def finish(o, cols):
        if final:
            o = o + ob_ref[:, cols]
            g = g_ref[:, cols]
            o = o * lax.rsqrt(jnp.mean(jnp.square(o), axis=-1, keepdims=True) + RMS_EPS)
            o = o * ng_ref[:, cols] * (g * jax.nn.sigmoid(g))
        return o.astype(o_ref.dtype)

    worst = jnp.zeros((), F32)
    for hh in range(heads):
        cols = slice(hh * LANES, (hh + 1) * LANES)
        q = q_ref[:, cols]
        v = i_ref[:, cols]
        z = f_ref[:, cols]
        lb = gconst_ref[0:1, cols]
        one_m_lb = gconst_ref[1:2, cols]

        e = jnp.exp(-jnp.abs(z))
        r = 1.0 / (1.0 + e)
        num = jnp.where(z >= 0.0, 1.0 + lb * e, lb + e)
        logf = jnp.log(jnp.maximum(num * r, F32_TINY))
        k = one_m_lb * jnp.where(z >= 0.0, e * r, r)
        v16 = v.astype(BF16)

        b_sub = logf
        for s in (1, 2, 4, 8):
            if rev:
                b_sub = b_sub + jnp.where(pos_in_sub < SUB - s, pltpu.roll(b_sub, tb - s, 0), 0.0)
            else:
                b_sub = b_sub + jnp.where(pos_in_sub >= s, pltpu.roll(b_sub, s, 0), 0.0)
        last = 0 if rev else SUB - 1
        zero_row = jnp.zeros((1, LANES), F32)
        before_sub = [None] * (n_chunks * n_sub)
        before_chunk = [None] * n_chunks
        run_block = zero_row
        for c in chunk_order:
            run = zero_row
            for j in sub_order:
                u = c * n_sub + j
                before_sub[u] = run
                run = run + b_sub[u * SUB + last:u * SUB + last + 1]
            before_chunk[c] = run_block
            run_block = run_block + run
        b_chunk = b_sub + rows_of(before_sub, SUB)
        b_blk = b_chunk + rows_of(before_chunk, CHUNK)

        q_sub = q * jnp.exp(b_sub)
        q_parts, k_parts = [], []
        for i in range(1, n_sub):
            j = sub_order[i]
            b_ref = rows_of([before_sub[c * n_sub + j] for c in range(n_chunks)], CHUNK)
            is_src = (sub_of_row > j) if rev else (sub_of_row < j)
            q_parts.append(jnp.where(sub_of_row == j, q_sub, 0.0).astype(BF16))
            k_parts.append(jnp.where(is_src, k * jnp.exp(jnp.minimum(b_ref - b_chunk, 0.0)), 0.0).astype(BF16))
        s_off = _dot_nt(jnp.concatenate(q_parts, axis=1), jnp.concatenate(k_parts, axis=1))
        a_off = jnp.where(code == OFF_CODE, s_off, 0.0)

        if n_chunks > 1:
            q_chunk = q * jnp.exp(b_chunk)
            q_parts, k_parts = [], []
            for i in range(1, n_chunks):
                c = chunk_order[i]
                is_src = (chunk_of_row > c) if rev else (chunk_of_row < c)
                q_parts.append(jnp.where(chunk_of_row == c, q_chunk, 0.0).astype(BF16))
                k_parts.append(
                    jnp.where(is_src, k * jnp.exp(jnp.minimum(before_chunk[c] - b_blk, 0.0)), 0.0).astype(BF16))
            a_off = a_off + _dot_nt(jnp.concatenate(q_parts, axis=1), jnp.concatenate(k_parts, axis=1))
        o = _dot(a_off.astype(BF16), v16)

        st = st_ref[hh]
        o = o + _dot_nt((q * jnp.exp(b_blk)).astype(BF16), st.astype(BF16))
        k_out = (k * jnp.exp(run_block - b_blk)).astype(BF16)
        st_ref[hh] = st * jnp.exp(run_block) + _dot_tn(v16, k_out)

        k_ref[:, cols] = k
        bsub_ref[:, cols] = b_sub
        rest_ref[:, cols] = o

        k_sub = (k * jnp.exp(-b_sub)).astype(BF16)
        s_diag = _dot_nt(q_sub.astype(BF16), k_sub)
        o = o + _dot(jnp.where(code == DIAG_CODE, s_diag, 0.0).astype(BF16), v16)
        worst = jnp.maximum(worst, jnp.max(-b_sub))
        o_ref[:, cols] = finish(o, cols)

    @pl.when(worst > SAFE_LOG_DECAY)
    def _():
        for hh in range(heads):
            cols = slice(hh * LANES, (hh + 1) * LANES)
            exact = diag_exact(q_ref[:, cols], k_ref[:, cols], i_ref[:, cols], bsub_ref[:, cols])
            o_ref[:, cols] = finish(rest_ref[:, cols] + exact, cols)

    @pl.when(step == pl.num_programs(2) - 1)
    def _():
        s_out_ref[...] = st_ref[...]


def _hgrn_scan(u, gconst, rev, init_state=None, final_inputs=None):
    b, t, _ = u.shape
    tb = min(256, t)
    nblk = t // tb
    final = final_inputs is not None
    has_init = init_state is not None
    code = _hgrn_pair_codes(tb, rev)
    heads = HGRN_HEADS_PER_STEP
    hw = heads * LANES
    row = (lambda s: nblk - 1 - s) if rev else (lambda s: s)

    def col_spec(col0):
        return pl.BlockSpec((None, tb, hw), lambda bi, h, s: (bi, row(s), col0 // hw + h))

    state_spec = pl.BlockSpec((None, heads, LANES, LANES), lambda bi, h, s: (bi, h, 0, 0))
    in_specs = [col_spec(COL_AQ), col_spec(COL_AI), col_spec(COL_AFB if rev else COL_AFF),
                pl.BlockSpec((2, hw), lambda bi, h, s: (0, h)),
                pl.BlockSpec((tb, tb), lambda bi, h, s: (0, 0))]
    args = [u, u, u, gconst, code]
    if has_init:
        in_specs.append(state_spec)
        args.append(init_state)
    if final:
        other, norm_g = final_inputs
        in_specs += [col_spec(COL_AG), col_spec(0), pl.BlockSpec((1, hw), lambda bi, h, s: (0, h))]
        args += [u, other, norm_g]
    return pl.pallas_call(
        functools.partial(_hgrn_kernel, rev=rev, final=final, has_init=has_init, tb=tb),
        grid=(b, HA // heads, nblk),
        in_specs=in_specs,
        out_specs=[col_spec(0), state_spec],
        out_shape=[jax.ShapeDtypeStruct((b, t, BRANCH_W), BF16 if final else F32),
                   jax.ShapeDtypeStruct((b, HA, LANES, LANES), F32)],
        scratch_shapes=[pltpu.VMEM((heads, LANES, LANES), F32),
                        pltpu.VMEM((tb, hw), F32), pltpu.VMEM((tb, hw), F32), pltpu.VMEM((tb, hw), F32)],
        compiler_params=_cparams(3),
        name="hgrn_fwd" if final else "hgrn_bwd",
    )(*args)


def _hgrn_mixer(uc, ul, gconst, norm_g, need_ctx):
    ob_c, sb = _hgrn_scan(uc, gconst[1], rev=True)
    ob_l, _ = _hgrn_scan(ul, gconst[1], rev=True, init_state=sb)
    if need_ctx:
        ya_c, sf = _hgrn_scan(uc, gconst[0], rev=False, final_inputs=(ob_c, norm_g))
    else:
        ya_c = None
        _, sf = _hgrn_scan(uc, gconst[0], rev=False)
    ya_l, _ = _hgrn_scan(ul, gconst[0], rev=False, init_state=sf, final_inputs=(ob_l, norm_g))
    return ya_c, ya_l


def _merge_kernel(ya_ref, yb_ref, yc_ref, wb_ref, g0_ref, g1_ref, g2_ref, o_ref):
    acc = jax.nn.sigmoid(g0_ref[...]) * _dot(ya_ref[...], wb_ref[0])
    acc = acc + jax.nn.sigmoid(g1_ref[...]) * _dot(yb_ref[...], wb_ref[1])
    acc = acc + jax.nn.sigmoid(g2_ref[...]) * _dot(yc_ref[...], wb_ref[2])
    o_ref[...] = acc.astype(o_ref.dtype)


def _merge_branches(u, ya, yb, yc, w_branch, layer, tm):
    b, t, _ = u.shape
    d = w_branch.shape[3]
    tn = 512
    y_spec = pl.BlockSpec((None, tm, BRANCH_W), lambda bi, i, j: (bi, i, 0))
    gate = lambda br: pl.BlockSpec(
        (None, tm, tn), lambda bi, i, j: (bi, i, (COL_GATE + br * d) // tn + j))
    return pl.pallas_call(
        _merge_kernel,
        grid=(b, t // tm, d // tn),
        in_specs=[y_spec, y_spec, y_spec,
                  pl.BlockSpec((None, N_BRANCH, BRANCH_W, tn), lambda bi, i, j: (layer, 0, 0, j)),
                  gate(0), gate(1), gate(2)],
        out_specs=pl.BlockSpec((None, tm, tn), lambda bi, i, j: (bi, i, j)),
        out_shape=jax.ShapeDtypeStruct((b, t, d), BF16),
        compiler_params=_cparams(3),
        name="merge_branches",
    )(ya, yb, yc, w_branch, u, u, u)


def _mmln_kernel(a_ref, w_ref, x_ref, mod_ref, g_ref, b_ref, o_ref, y_ref, *, gate_idx, nj, tn, alpha):
    j = pl.program_id(2)
    y_ref[j] = _dot(a_ref[...], w_ref[...])

    @pl.when(j == nj - 1)
    def _():
        d = nj * tn
        total = jnp.zeros((a_ref.shape[0], 1), F32)
        for jj in range(nj):
            cols = slice(jj * tn, (jj + 1) * tn)
            res = alpha * x_ref[:, cols] + mod_ref[gate_idx:gate_idx + 1, cols] * y_ref[jj]
            y_ref[jj] = res
            total = total + jnp.sum(res, axis=-1, keepdims=True)
        mu = total / d
        sq = jnp.zeros_like(total)
        for jj in range(nj):
            sq = sq + jnp.sum(jnp.square(y_ref[jj] - mu), axis=-1, keepdims=True)
        inv = lax.rsqrt(sq / d + LN_EPS)
        for jj in range(nj):
            cols = slice(jj * tn, (jj + 1) * tn)
            o_ref[:, cols] = (y_ref[jj] - mu) * inv * g_ref[:, cols] + b_ref[:, cols]


def _matmul_res_ln(a, w, layer, x, mod, ln_g, ln_b, *, ctx_row, gate_idx, alpha, tm, tn, name):
    b, t, kdim = a.shape
    d = w.shape[2]
    nj = d // tn
    row = (lambda bi: MOD_ROWS // 2) if ctx_row else (lambda bi: bi)
    return pl.pallas_call(
        functools.partial(_mmln_kernel, gate_idx=gate_idx, nj=nj, tn=tn, alpha=alpha),
        grid=(b, t // tm, nj),
        in_specs=[
            pl.BlockSpec((None, tm, kdim), lambda bi, i, j: (bi, i, 0)),
            pl.BlockSpec((None, kdim, tn), lambda bi, i, j: (layer, 0, j)),
            pl.BlockSpec((None, tm, d), lambda bi, i, j: (bi, i, 0)),
            pl.BlockSpec((None, N_ADA, d), lambda bi, i, j: (row(bi), 0, 0)),
            pl.BlockSpec((1, d), lambda bi, i, j: (0, 0)),
            pl.BlockSpec((1, d), lambda bi, i, j: (0, 0)),
        ],
        out_specs=pl.BlockSpec((None, tm, d), lambda bi, i, j: (bi, i, 0)),
        out_shape=jax.ShapeDtypeStruct((b, t, d), F32),
        scratch_shapes=[pltpu.VMEM((nj, tm, tn), F32)],
        compiler_params=_cparams(3),
        name=name,
    )(a, w, x, mod, ln_g, ln_b)


def _rope_tables(t_len):
    rows = t_len // GRID_W
    row = jnp.repeat(jnp.arange(rows, dtype=F32), GRID_W)
    col = jnp.tile(jnp.arange(GRID_W, dtype=F32), rows)
    n_freq = HD // 4
    inv_freq = ROPE_THETA ** (-jnp.arange(n_freq, dtype=F32) / n_freq)
    ang = jnp.concatenate([row[:, None] * inv_freq, col[:, None] * inv_freq], -1)
    cos, sin = jnp.cos(ang), jnp.sin(ang)
    return jnp.concatenate([cos, cos], -1), jnp.concatenate([-sin, sin], -1)


def kernel(x, c, ctx, c_ctx, w_ada, b_ada, w_in, hgrn_lb, hgrn_norm, pool_w, pool_scale, attn_sink, w_branch,
           w_out, ln1_g, ln1_b, w_ff1, w_ff2, ln2_g, ln2_b):
    depth = w_ada.shape[0]
    bsz, t_len, d = x.shape
    tc_len = ctx.shape[1]
    assert d == D_MODEL and w_in.shape[2] == N_COLS and bsz <= MOD_ROWS // 2
    alpha = (2.0 * depth) ** 0.25

    lb = jnp.cumsum(jax.nn.softmax(hgrn_lb.astype(F32), axis=0), axis=0)
    lb = lb - lb[0:1]
    gconst = jnp.stack([lb, 1.0 - lb], axis=2)

    cc = jnp.zeros((MOD_ROWS, d), F32).at[:bsz].set(c).at[MOD_ROWS // 2].set(c_ctx)
    mod = _modulation(cc, w_ada, b_ada).reshape(depth, MOD_ROWS, N_ADA, d)
    cos2, sin2 = _rope_tables(t_len)

    n_ctx = bsz * tc_len
    tm_l = min(1024, t_len)
    tm_c = min(1024, n_ctx)
    tm_ff_l = min(512, t_len)
    tm_ff_c = min(512, n_ctx)
    d_ff = w_ff1.shape[2]

    w_in_b, w_br_b, w_out_b = w_in.astype(BF16), w_branch.astype(BF16), w_out.astype(BF16)
    w_ff1_b, w_ff2_b, pool_w_b = w_ff1.astype(BF16), w_ff2.astype(BF16), pool_w.astype(BF16)

    def flat(a):
        return a.reshape(1, n_ctx, a.shape[-1])

    xc, xl = flat(ctx), x
    for l in range(depth):
        need_ctx = l < depth - 1
        mod_l = mod[l]
        g1, b1 = ln1_g[l].reshape(1, d), ln1_b[l].reshape(1, d)
        g2, b2 = ln2_g[l].reshape(1, d), ln2_b[l].reshape(1, d)
        sink_l = attn_sink[l].astype(F32)
        scale_l = pool_scale[l].reshape(1, -1)
        norm_l = hgrn_norm[l].reshape(1, -1)

        in_proj = functools.partial(_mod_matmul, sh_idx=0, sc_idx=1, sq_relu=False, out_dtype=F32, tn=768)
        ul = in_proj(xl, mod_l, w_in_b, l, ctx_row=False, tm=tm_l, name="in_proj_latent")
        uc_flat = in_proj(xc, mod_l, w_in_b, l, ctx_row=True, tm=tm_c, name="in_proj_ctx")
        uc = uc_flat.reshape(bsz, tc_len, N_COLS)

        ya_c, ya_l = _hgrn_mixer(uc, ul, gconst[l], norm_l, need_ctx)
        yb_l = _pool_mixer(ul, pool_w_b, l, scale_l)
        q_r, kv_r = _rope_qkv(ul, cos2, sin2)
        yc_l = _attention_latent(sink_l, q_r, kv_r, uc)

        def finish_stream(xs, us, ya, yb, yc, ctx_row, tm, tm_ff, l=l, mod_l=mod_l, g1=g1, b1=b1, g2=g2, b2=b2):
            merged = _merge_branches(us, ya, yb, yc, w_br_b, l, tm)
            xs = _matmul_res_ln(merged, w_out_b, l, xs, mod_l, g1, b1, ctx_row=ctx_row, gate_idx=2, alpha=alpha,
                                tm=tm_ff, tn=d, name="out_proj_ln")
            hidden = _mod_matmul(xs, mod_l, w_ff1_b, l, ctx_row=ctx_row, sh_idx=3, sc_idx=4, sq_relu=True,
                                 out_dtype=BF16, tm=tm, tn=1024, name="ffn_up")
            return _matmul_res_ln(hidden, w_ff2_b, l, xs, mod_l, g2, b2, ctx_row=ctx_row, gate_idx=5,
                                  alpha=alpha, tm=tm_ff, tn=256, name="ffn_down_ln")

        xl = finish_stream(xl, ul, ya_l, yb_l, yc_l, False, tm_l, tm_ff_l)
        if need_ctx:
            yb_c = _pool_mixer(uc, pool_w_b, l, scale_l)
            yc_c = _attention_ctx(sink_l, uc)
            xc = finish_stream(xc, uc_flat, flat(ya_c), flat(yb_c), flat(yc_c), True, tm_c, tm_ff_c)
    return xl
```

```python
import functools

import jax
import jax.numpy as jnp
from jax import lax
from jax.experimental import pallas as pl
from jax.experimental.pallas import tpu as pltpu

F32 = jnp.float32
BF16 = jnp.bfloat16

D_MODEL = 2048
GRID_W = 64
HA, DK_A = 8, 128
WK_A = HA * DK_A
POOL_WINDOWS = (2, 4, 8, 16)
POOL_GW = 256
HQ, HKV, HD = 8, 2, 128
GQ = HQ // HKV
WINDOW = 128
BLOCK_Q = 128
ROPE_THETA = 10000.0
N_BRANCH = 3
BRANCH_W = 1024
N_ADA = 6
LN_EPS = 1e-5
RMS_EPS = 1e-6

COL_AQ = 0
COL_AI = COL_AQ + WK_A
COL_AG = COL_AI + BRANCH_W
COL_AFF = COL_AG + BRANCH_W
COL_AFB = COL_AFF + WK_A
COL_B = COL_AFB + WK_A
COL_CQ = COL_B + BRANCH_W
COL_CK = COL_CQ + BRANCH_W
COL_CV = COL_CK + HKV * HD
COL_GATE = COL_CV + HKV * HD
N_COLS = COL_GATE + N_BRANCH * D_MODEL

LANES = 128
VMEM_LIMIT = 56 * 1024 * 1024

CHUNK = 64
SUB = 16
HGRN_HEADS_PER_STEP = 2
MOD_ROWS = 16


def _cparams(n_axes):
    return pltpu.CompilerParams(dimension_semantics=("arbitrary",) * n_axes, vmem_limit_bytes=VMEM_LIMIT)


def _dot(a, b):
    return jnp.dot(a, b, preferred_element_type=F32)


def _dot_nt(a, b):
    return lax.dot_general(a, b, (((1,), (1,)), ((), ())), preferred_element_type=F32)


def _dot_tn(a, b):
    return lax.dot_general(a, b, (((0,), (0,)), ((), ())), preferred_element_type=F32)


def _mod_kernel(c_ref, w_ref, b_ref, o_ref):
    c = c_ref[...]
    s = (c * jax.nn.sigmoid(c)).astype(BF16)
    o_ref[...] = _dot(s, w_ref[...].astype(BF16)) + b_ref[...]


def _modulation(cc, w_ada, b_ada):
    depth, d, n = w_ada.shape
    tn = 1024
    return pl.pallas_call(
        _mod_kernel,
        grid=(depth, n // tn),
        in_specs=[
            pl.BlockSpec((MOD_ROWS, d), lambda l, j: (0, 0)),
            pl.BlockSpec((None, d, tn), lambda l, j: (l, 0, j)),
            pl.BlockSpec((None, 1, tn), lambda l, j: (l, 0, j)),
        ],
        out_specs=pl.BlockSpec((None, MOD_ROWS, tn), lambda l, j: (l, 0, j)),
        out_shape=jax.ShapeDtypeStruct((depth, MOD_ROWS, n), F32),
        compiler_params=_cparams(2),
        name="adaln_mod",
    )(cc, w_ada, b_ada.reshape(depth, 1, n))


def _modmm_kernel(x_ref, mod_ref, w_ref, o_ref, xm_ref, *, sh_idx, sc_idx, sq_relu):
    @pl.when(pl.program_id(2) == 0)
    def _():
        sh = mod_ref[sh_idx:sh_idx + 1, :]
        sc = mod_ref[sc_idx:sc_idx + 1, :]
        xm_ref[...] = (x_ref[...] * (1.0 + sc) + sh).astype(BF16)

    y = _dot(xm_ref[...], w_ref[...].astype(BF16))
    if sq_relu:
        y = jnp.square(jnp.maximum(y, 0.0))
    o_ref[...] = y.astype(o_ref.dtype)


def _mod_matmul(x, mod, w, layer, *, ctx_row, sh_idx, sc_idx, sq_relu, out_dtype, tm, tn, name):
    b, t, d = x.shape
    n = w.shape[2]
    row = (lambda bi: MOD_ROWS // 2) if ctx_row else (lambda bi: bi)
    return pl.pallas_call(
        functools.partial(_modmm_kernel, sh_idx=sh_idx, sc_idx=sc_idx, sq_relu=sq_relu),
        grid=(b, t // tm, n // tn),
        in_specs=[
            pl.BlockSpec((None, tm, d), lambda bi, i, j: (bi, i, 0)),
            pl.BlockSpec((None, N_ADA, d), lambda bi, i, j: (row(bi), 0, 0)),
            pl.BlockSpec((None, d, tn), lambda bi, i, j: (layer, 0, j)),
        ],
        out_specs=pl.BlockSpec((None, tm, tn), lambda bi, i, j: (bi, i, j)),
        out_shape=jax.ShapeDtypeStruct((b, t, n), out_dtype),
        scratch_shapes=[pltpu.VMEM((tm, d), BF16)],
        compiler_params=_cparams(3),
        name=name,
    )(x, mod, w)


def _rope_kernel(q_ref, kv_ref, cos_ref, sin_ref, qo_ref, kvo_ref):
    cos = cos_ref[...]
    sin = sin_ref[...]

    def rope(v):
        return v * cos + pltpu.roll(v, HD // 2, 1) * sin

    for h in range(HQ):
        sl = slice(h * HD, (h + 1) * HD)
        qo_ref[:, sl] = rope(q_ref[:, sl]).astype(BF16)
    for h in range(HKV):
        sl = slice(h * HD, (h + 1) * HD)
        kvo_ref[:, sl] = rope(kv_ref[:, sl]).astype(BF16)
    kvo_ref[:, HKV * HD:] = kv_ref[:, HKV * HD:].astype(BF16)


def _rope_qkv(u, cos2, sin2):
    b, t, _ = u.shape
    tq = min(512, t)
    wq, wkv = HQ * HD, 2 * HKV * HD
    return pl.pallas_call(
        _rope_kernel,
        grid=(b, t // tq),
        in_specs=[
            pl.BlockSpec((None, tq, wq), lambda bi, i: (bi, i, COL_CQ // wq)),
            pl.BlockSpec((None, tq, wkv), lambda bi, i: (bi, i, COL_CK // wkv)),
            pl.BlockSpec((tq, HD), lambda bi, i: (i, 0)),
            pl.BlockSpec((tq, HD), lambda bi, i: (i, 0)),
        ],
        out_specs=[
            pl.BlockSpec((None, tq, wq), lambda bi, i: (bi, i, 0)),
            pl.BlockSpec((None, tq, wkv), lambda bi, i: (bi, i, 0)),
        ],
        out_shape=[jax.ShapeDtypeStruct((b, t, wq), BF16), jax.ShapeDtypeStruct((b, t, wkv), BF16)],
        compiler_params=_cparams(2),
        name="rope_qkv",
    )(u, u, cos2, sin2)


MASKED = -1e30


def _softmax_pv_heads(s, bias, sink_ref, kv, vals, o_ref):
    rows = s.shape[0] // GQ
    for g in range(GQ):
        sg = s[g * rows:(g + 1) * rows] * (HD ** -0.5)
        if bias is not None:
            sg = sg + bias
        sink = sink_ref[kv * GQ + g]
        m = jnp.maximum(jnp.max(sg, axis=-1, keepdims=True), sink)
        e = jnp.exp(sg - m)
        denom = jnp.sum(e, axis=-1, keepdims=True) + jnp.exp(sink - m)
        o_ref[:, g * HD:(g + 1) * HD] = (_dot(e.astype(BF16), vals) * (1.0 / denom)).astype(o_ref.dtype)


def _attn_latent_kernel(sink_ref, bias_ref, q_ref, kl_ref, km_ref, kr_ref, vl_ref, vm_ref, vr_ref, kc_ref,
                        vc_ref, o_ref):
    kv = pl.program_id(1)
    q = jnp.concatenate([q_ref[:, g * HD:(g + 1) * HD] for g in range(GQ)], axis=0)
    keys = jnp.concatenate([kl_ref[...], km_ref[...], kr_ref[...], kc_ref[...].astype(BF16)], axis=0)
    vals = jnp.concatenate([vl_ref[...], vm_ref[...], vr_ref[...], vc_ref[...].astype(BF16)], axis=0)
    _softmax_pv_heads(_dot_nt(q, keys), bias_ref[...], sink_ref, kv, vals, o_ref)


def _band_bias(t_len, tc):
    bq = BLOCK_Q
    r = jnp.arange(bq)[:, None]
    c = jnp.arange(3 * bq)[None, :]
    in_band = jnp.abs(r - (c - bq)) <= WINDOW
    kinds = []
    for first, last in ((False, False), (True, False), (False, True), (True, True)):
        ok = in_band
        if first:
            ok = ok & (c >= bq)
        if last:
            ok = ok & (c < 2 * bq)
        kinds.append(jnp.concatenate([jnp.where(ok, 0.0, MASKED), jnp.zeros((bq, tc))], axis=1))
    return jnp.stack(kinds).astype(F32)


def _attention_latent(sink, q_r, kv_r, uc):
    b, t, _ = q_r.shape
    tc = uc.shape[1]
    nb = t // BLOCK_Q
    wq = GQ * HD
    nkeys = 3 * BLOCK_Q + tc
    kblk = lambda off: pl.BlockSpec(
        (None, BLOCK_Q, HD), lambda bi, kv, n: (bi, jnp.clip(n + off, 0, nb - 1), kv))
    vblk = lambda off: pl.BlockSpec(
        (None, BLOCK_Q, HD), lambda bi, kv, n: (bi, jnp.clip(n + off, 0, nb - 1), HKV + kv))
    kind = lambda n: (n == 0).astype(jnp.int32) + 2 * (n == nb - 1).astype(jnp.int32)
    return pl.pallas_call(
        _attn_latent_kernel,
        grid=(b, HKV, nb),
        in_specs=[
            pl.BlockSpec(memory_space=pltpu.SMEM),
            pl.BlockSpec((None, BLOCK_Q, nkeys), lambda bi, kv, n: (kind(n), 0, 0)),
            pl.BlockSpec((None, BLOCK_Q, wq), lambda bi, kv, n: (bi, n, kv)),
            kblk(-1), kblk(0), kblk(1), vblk(-1), vblk(0), vblk(1),
            pl.BlockSpec((None, tc, HD), lambda bi, kv, n: (bi, 0, COL_CK // HD + kv)),
            pl.BlockSpec((None, tc, HD), lambda bi, kv, n: (bi, 0, COL_CV // HD + kv)),
        ],
        out_specs=pl.BlockSpec((None, BLOCK_Q, wq), lambda bi, kv, n: (bi, n, kv)),
        out_shape=jax.ShapeDtypeStruct((b, t, HQ * HD), BF16),
        compiler_params=_cparams(3),
        name="attn_latent",
    )(sink, _band_bias(t, tc), q_r, kv_r, kv_r, kv_r, kv_r, kv_r, kv_r, uc, uc)


def _attn_ctx_kernel(sink_ref, q_ref, kc_ref, vc_ref, o_ref):
    kv = pl.program_id(1)
    q = jnp.concatenate([q_ref[:, g * HD:(g + 1) * HD] for g in range(GQ)], axis=0).astype(BF16)
    s = _dot_nt(q, kc_ref[...].astype(BF16))
    _softmax_pv_heads(s, None, sink_ref, kv, vc_ref[...].astype(BF16), o_ref)


def _attention_ctx(sink, uc):
    b, tc, _ = uc.shape
    wq = GQ * HD
    return pl.pallas_call(
        _attn_ctx_kernel,
        grid=(b, HKV),
        in_specs=[
            pl.BlockSpec(memory_space=pltpu.SMEM),
            pl.BlockSpec((None, tc, wq), lambda bi, kv: (bi, 0, COL_CQ // wq + kv)),
            pl.BlockSpec((None, tc, HD), lambda bi, kv: (bi, 0, COL_CK // HD + kv)),
            pl.BlockSpec((None, tc, HD), lambda bi, kv: (bi, 0, COL_CV // HD + kv)),
        ],
        out_specs=pl.BlockSpec((None, tc, wq), lambda bi, kv: (bi, 0, kv)),
        out_shape=jax.ShapeDtypeStruct((b, tc, HQ * HD), BF16),
        compiler_params=_cparams(2),
        name="attn_ctx",
    )(sink, uc, uc, uc)


def _pool_kernel(u_ref, w_ref, sc_ref, o_ref):
    t_len = u_ref.shape[0]
    grp = pl.program_id(1)
    x = u_ref[...]
    t = lax.broadcasted_iota(jnp.int32, x.shape, 0)

    def shift_down(a, k):
        return jnp.where(t >= k, pltpu.roll(a, k, 0), 0.0)

    def shift_up(a, k):
        return jnp.where(t < t_len - k, pltpu.roll(a, t_len - k, 0), 0.0)

    for j, win in enumerate(POOL_WINDOWS):
        @pl.when(grp == j)
        def _(win=win):
            half = win // 2
            trail, lead, step = x, x, 1
            while step < half:
                trail = trail + shift_down(trail, step)
                lead = lead + shift_up(lead, step)
                step *= 2
            total = shift_down(trail, 1) + lead
            cnt = (jnp.minimum(t + half, t_len) - jnp.maximum(t - half, 0)).astype(F32)
            y = _dot((total / cnt - x).astype(BF16), w_ref[...])
            o_ref[...] = (y * sc_ref[...]).astype(o_ref.dtype)


def _pool_mixer(u, pool_w, layer, pool_scale):
    b, t, _ = u.shape
    ng = len(POOL_WINDOWS)
    return pl.pallas_call(
        _pool_kernel,
        grid=(b, ng),
        in_specs=[
            pl.BlockSpec((None, t, POOL_GW), lambda bi, g: (bi, 0, COL_B // POOL_GW + g)),
            pl.BlockSpec((None, None, POOL_GW, POOL_GW), lambda bi, g: (layer, g, 0, 0)),
            pl.BlockSpec((1, POOL_GW), lambda bi, g: (0, g)),
        ],
        out_specs=pl.BlockSpec((None, t, POOL_GW), lambda bi, g: (bi, 0, g)),
        out_shape=jax.ShapeDtypeStruct((b, t, ng * POOL_GW), BF16),
        compiler_params=_cparams(2),
        name="pool_mixer",
    )(u, pool_w, pool_scale)


SAFE_LOG_DECAY = 80.0
F32_TINY = 1.1754944e-38
DIAG_CODE, OFF_CODE = 2.0, 1.0


def _hgrn_pair_tables(tb, rev):
    r = jnp.arange(tb)[:, None]
    c = jnp.arange(tb)[None, :]
    ordered = (c >= r) if rev else (c <= r)
    same_chunk = (r // CHUNK) == (c // CHUNK)
    same_sub = (r // SUB) == (c // SUB)
    code = jnp.where(same_sub & ordered, DIAG_CODE, jnp.where(same_chunk & ordered, OFF_CODE, 0.0)).astype(F32)
    return code, (same_sub & ordered).astype(BF16)


def _hgrn_kernel(*refs, rev, final, has_init, tb):
    refs = list(refs)
    q_ref, i_ref, f_ref, gconst_ref, code_ref, cum_ref = refs[:6]
    del refs[:6]
    s0_ref = refs.pop(0) if has_init else None
    if final:
        g_ref, ob_ref, ng_ref = refs[:3]
        del refs[:3]
    o_ref, s_out_ref, st_ref, k_ref, bsub_ref, ostate_ref, aoff_ref = refs
    step = pl.program_id(2)
    n_chunks = tb // CHUNK
    n_sub = CHUNK // SUB
    heads = st_ref.shape[0]

    @pl.when(step == 0)
    def _():
        st_ref[...] = s0_ref[...] if has_init else jnp.zeros_like(st_ref)

    code = code_ref[...]
    cum = cum_ref[...]
    zero_row = jnp.zeros((1, LANES), F32)
    row = lax.broadcasted_iota(jnp.int32, (tb, LANES), 0)
    pos_in_sub = row % SUB
    sub_of_row = (row % CHUNK) // SUB
    chunk_of_row = row // CHUNK
    sub_order = list(reversed(range(n_sub))) if rev else list(range(n_sub))
    chunk_order = list(reversed(range(n_chunks))) if rev else list(range(n_chunks))

    def rows_of(values, height):
        return jnp.concatenate([jnp.broadcast_to(val, (height, LANES)) for val in values], axis=0)

    def diag_exact(q, k, v, b_sub):
        pos = pos_in_sub
        acc = jnp.zeros((tb, LANES), F32)
        for d in range(SUB):
            if d == 0:
                kd, bd, vd = k, b_sub, v
            else:
                sh = tb - d if rev else d
                kd, bd, vd = pltpu.roll(k, sh, 0), pltpu.roll(b_sub, sh, 0), pltpu.roll(v, sh, 0)
            a = jnp.sum(q * kd * jnp.exp(jnp.minimum(b_sub - bd, 0.0)), axis=-1, keepdims=True)
            valid = (pos <= SUB - 1 - d) if rev else (pos >= d)
            acc = acc + jnp.where(valid, a, 0.0) * vd
        return acc

    def finish(o, cols):
        if final:
            o = o + ob_ref[:, cols]
            g = g_ref[:, cols]
            o = o * lax.rsqrt(jnp.mean(jnp.square(o), axis=-1, keepdims=True) + RMS_EPS)
            o = o * ng_ref[:, cols] * (g * jax.nn.sigmoid(g))
        return o.astype(o_ref.dtype)

    def stage_gates(hh):
        cols = slice(hh * LANES, (hh + 1) * LANES)
        q = q_ref[:, cols]
        v = i_ref[:, cols]
        z = f_ref[:, cols]
        lb = gconst_ref[0:1, cols]
        one_m_lb = gconst_ref[1:2, cols]

        e = jnp.exp(-jnp.abs(z))
        r = 1.0 / (1.0 + e)
        num = jnp.where(z >= 0.0, 1.0 + lb * e, lb + e)
        logf = jnp.log(jnp.maximum(num * r, F32_TINY))
        k = one_m_lb * jnp.where(z >= 0.0, e * r, r)
        v16 = v.astype(BF16)

        hi = logf.astype(BF16)
        lo = (logf - hi.astype(F32)).astype(BF16)
        both = _dot(cum, jnp.concatenate([hi, lo], axis=1))
        return q, k, v16, both

    def stage_scores(hh, q, k, v16, both):
        cols = slice(hh * LANES, (hh + 1) * LANES)
        b_sub = both[:, :LANES] + both[:, LANES:]
        last = 0 if rev else SUB - 1
        n_units = n_chunks * n_sub
        tot_sub = [b_sub[u * SUB + last:u * SUB + last + 1] for u in range(n_units)]
        before_sub = [None] * n_units
        before_chunk = [None] * n_chunks
        tot_chunk = [None] * n_chunks
        run_block = zero_row
        for c in chunk_order:
            run = zero_row
            for j in sub_order:
                before_sub[c * n_sub + j] = run
                run = run + tot_sub[c * n_sub + j]
            before_chunk[c] = run_block
            tot_chunk[c] = run
            run_block = run_block + run
        end_sub = [before_sub[u] + tot_sub[u] for u in range(n_units)]
        end_chunk = [before_chunk[c] + tot_chunk[c] for c in range(n_chunks)]
        b_chunk = b_sub + rows_of(before_sub, SUB)
        b_blk = b_chunk + rows_of(before_chunk, CHUNK)

        q_sub = q * jnp.exp(b_sub)
        k_to_sub_end = k * jnp.exp(rows_of(tot_sub, SUB) - b_sub)
        q_parts, k_parts = [], []
        for i in range(1, n_sub):
            j = sub_order[i]
            scale = [jnp.exp(before_sub[c * n_sub + j] - end_sub[c * n_sub + js]) if sub_order.index(js) < i
                     else zero_row for c in range(n_chunks) for js in range(n_sub)]
            q_parts.append(jnp.where(sub_of_row == j, q_sub, 0.0).astype(BF16))
            k_parts.append((k_to_sub_end * rows_of(scale, SUB)).astype(BF16))
        s_off = _dot_nt(jnp.concatenate(q_parts, axis=1), jnp.concatenate(k_parts, axis=1))
        a_off = jnp.where(code == OFF_CODE, s_off, 0.0)

        k_to_chunk_end = k * jnp.exp(rows_of(tot_chunk, CHUNK) - b_chunk)
        if n_chunks > 1:
            q_chunk = q * jnp.exp(b_chunk)
            q_parts, k_parts = [], []
            for i in range(1, n_chunks):
                c = chunk_order[i]
                scale = [jnp.exp(before_chunk[c] - end_chunk[cs]) if chunk_order.index(cs) < i else zero_row
                         for cs in range(n_chunks)]
                q_parts.append(jnp.where(chunk_of_row == c, q_chunk, 0.0).astype(BF16))
                k_parts.append((k_to_chunk_end * rows_of(scale, CHUNK)).astype(BF16))
            a_off = a_off + _dot_nt(jnp.concatenate(q_parts, axis=1), jnp.concatenate(k_parts, axis=1))
        a_off = a_off.astype(BF16)

        st = st_ref[hh]
        o_state = _dot_nt((q * jnp.exp(b_blk)).astype(BF16), st.astype(BF16))
        to_block_end = rows_of([jnp.exp(run_block - end_chunk[c]) for c in range(n_chunks)], CHUNK)
        st_ref[hh] = st * jnp.exp(run_block) + _dot_tn(v16, (k_to_chunk_end * to_block_end).astype(BF16))

        k_ref[:, cols] = k
        bsub_ref[:, cols] = b_sub
        ostate_ref[:, cols] = o_state
        aoff_ref[hh] = a_off

        k_sub = (k * jnp.exp(-b_sub)).astype(BF16)
        s_diag = _dot_nt(q_sub.astype(BF16), k_sub)
        return a_off, s_diag, o_state, jnp.max(-b_sub)

    def stage_mix(v16, a_off, s_diag):
        a_all = a_off + jnp.where(code == DIAG_CODE, s_diag, 0.0).astype(BF16)
        return _dot(a_all, v16)

    gated = [stage_gates(hh) for hh in range(heads)]
    scored = [stage_scores(hh, *gated[hh]) for hh in range(heads)]
    mixed = [stage_mix(gated[hh][2], scored[hh][0], scored[hh][1]) for hh in range(heads)]
    worst = jnp.zeros((), F32)
    for hh in range(heads):
        cols = slice(hh * LANES, (hh + 1) * LANES)
        worst = jnp.maximum(worst, scored[hh][3])
        o_ref[:, cols] = finish(mixed[hh] + scored[hh][2], cols)

    @pl.when(worst > SAFE_LOG_DECAY)
    def _():
        for hh in range(heads):
            cols = slice(hh * LANES, (hh + 1) * LANES)
            v = i_ref[:, cols]
            exact = diag_exact(q_ref[:, cols], k_ref[:, cols], v, bsub_ref[:, cols])
            o_ref[:, cols] = finish(_dot(aoff_ref[hh], v.astype(BF16)) + ostate_ref[:, cols] + exact, cols)

    @pl.when(step == pl.num_programs(2) - 1)
    def _():
        s_out_ref[...] = st_ref[...]


def _hgrn_scan(u, gconst, rev, init_state=None, final_inputs=None):
    b, t, _ = u.shape
    tb = min(256, t)
    nblk = t // tb
    final = final_inputs is not None
    has_init = init_state is not None
    code, cum = _hgrn_pair_tables(tb, rev)
    heads = HGRN_HEADS_PER_STEP
    hw = heads * LANES
    row = (lambda s: nblk - 1 - s) if rev else (lambda s: s)

    def col_spec(col0):
        return pl.BlockSpec((None, tb, hw), lambda bi, h, s: (bi, row(s), col0 // hw + h))

    state_spec = pl.BlockSpec((None, heads, LANES, LANES), lambda bi, h, s: (bi, h, 0, 0))
    in_specs = [col_spec(COL_AQ), col_spec(COL_AI), col_spec(COL_AFB if rev else COL_AFF),
                pl.BlockSpec((2, hw), lambda bi, h, s: (0, h)),
                pl.BlockSpec((tb, tb), lambda bi, h, s: (0, 0)),
                pl.BlockSpec((tb, tb), lambda bi, h, s: (0, 0))]
    args = [u, u, u, gconst, code, cum]
    if has_init:
        in_specs.append(state_spec)
        args.append(init_state)
    if final:
        other, norm_g = final_inputs
        in_specs += [col_spec(COL_AG), col_spec(0), pl.BlockSpec((1, hw), lambda bi, h, s: (0, h))]
        args += [u, other, norm_g]
    return pl.pallas_call(
        functools.partial(_hgrn_kernel, rev=rev, final=final, has_init=has_init, tb=tb),
        grid=(b, HA // heads, nblk),
        in_specs=in_specs,
        out_specs=[col_spec(0), state_spec],
        out_shape=[jax.ShapeDtypeStruct((b, t, BRANCH_W), BF16 if final else F32),
                   jax.ShapeDtypeStruct((b, HA, LANES, LANES), F32)],
        scratch_shapes=[pltpu.VMEM((heads, LANES, LANES), F32),
                        pltpu.VMEM((tb, hw), F32), pltpu.VMEM((tb, hw), F32), pltpu.VMEM((tb, hw), F32),
                        pltpu.VMEM((heads, tb, tb), BF16)],
        compiler_params=_cparams(3),
        name="hgrn_fwd" if final else "hgrn_bwd",
    )(*args)


def _hgrn_mixer(uc, ul, gconst, norm_g, need_ctx):
    ob_c, sb = _hgrn_scan(uc, gconst[1], rev=True)
    ob_l, _ = _hgrn_scan(ul, gconst[1], rev=True, init_state=sb)
    if need_ctx:
        ya_c, sf = _hgrn_scan(uc, gconst[0], rev=False, final_inputs=(ob_c, norm_g))
    else:
        ya_c = None
        _, sf = _hgrn_scan(uc, gconst[0], rev=False)
    ya_l, _ = _hgrn_scan(ul, gconst[0], rev=False, init_state=sf, final_inputs=(ob_l, norm_g))
    return ya_c, ya_l


def _merge_kernel(ya_ref, yb_ref, yc_ref, wb_ref, g0_ref, g1_ref, g2_ref, o_ref):
    acc = jax.nn.sigmoid(g0_ref[...]) * _dot(ya_ref[...], wb_ref[0])
    acc = acc + jax.nn.sigmoid(g1_ref[...]) * _dot(yb_ref[...], wb_ref[1])
    acc = acc + jax.nn.sigmoid(g2_ref[...]) * _dot(yc_ref[...], wb_ref[2])
    o_ref[...] = acc.astype(o_ref.dtype)


def _merge_branches(u, ya, yb, yc, w_branch, layer, tm):
    b, t, _ = u.shape
    d = w_branch.shape[3]
    tn = 512
    y_spec = pl.BlockSpec((None, tm, BRANCH_W), lambda bi, i, j: (bi, i, 0))
    gate = lambda br: pl.BlockSpec(
        (None, tm, tn), lambda bi, i, j: (bi, i, (COL_GATE + br * d) // tn + j))
    return pl.pallas_call(
        _merge_kernel,
        grid=(b, t // tm, d // tn),
        in_specs=[y_spec, y_spec, y_spec,
                  pl.BlockSpec((None, N_BRANCH, BRANCH_W, tn), lambda bi, i, j: (layer, 0, 0, j)),
                  gate(0), gate(1), gate(2)],
        out_specs=pl.BlockSpec((None, tm, tn), lambda bi, i, j: (bi, i, j)),
        out_shape=jax.ShapeDtypeStruct((b, t, d), BF16),
        compiler_params=_cparams(3),
        name="merge_branches",
    )(ya, yb, yc, w_branch, u, u, u)


def _mmln_kernel(a_ref, w_ref, x_ref, mod_ref, g_ref, b_ref, o_ref, y_ref, *, gate_idx, nj, tn, alpha):
    j = pl.program_id(2)
    y_ref[j] = _dot(a_ref[...], w_ref[...])

    @pl.when(j == nj - 1)
    def _():
        d = nj * tn
        total = jnp.zeros((a_ref.shape[0], 1), F32)
        for jj in range(nj):
            cols = slice(jj * tn, (jj + 1) * tn)
            res = alpha * x_ref[:, cols] + mod_ref[gate_idx:gate_idx + 1, cols] * y_ref[jj]
            y_ref[jj] = res
            total = total + jnp.sum(res, axis=-1, keepdims=True)
        mu = total / d
        sq = jnp.zeros_like(total)
        for jj in range(nj):
            sq = sq + jnp.sum(jnp.square(y_ref[jj] - mu), axis=-1, keepdims=True)
        inv = lax.rsqrt(sq / d + LN_EPS)
        for jj in range(nj):
            cols = slice(jj * tn, (jj + 1) * tn)
            o_ref[:, cols] = (y_ref[jj] - mu) * inv * g_ref[:, cols] + b_ref[:, cols]


def _matmul_res_ln(a, w, layer, x, mod, ln_g, ln_b, *, ctx_row, gate_idx, alpha, tm, tn, name):
    b, t, kdim = a.shape
    d = w.shape[2]
    nj = d // tn
    row = (lambda bi: MOD_ROWS // 2) if ctx_row else (lambda bi: bi)
    return pl.pallas_call(
        functools.partial(_mmln_kernel, gate_idx=gate_idx, nj=nj, tn=tn, alpha=alpha),
        grid=(b, t // tm, nj),
        in_specs=[
            pl.BlockSpec((None, tm, kdim), lambda bi, i, j: (bi, i, 0)),
            pl.BlockSpec((None, kdim, tn), lambda bi, i, j: (layer, 0, j)),
            pl.BlockSpec((None, tm, d), lambda bi, i, j: (bi, i, 0)),
            pl.BlockSpec((None, N_ADA, d), lambda bi, i, j: (row(bi), 0, 0)),
            pl.BlockSpec((1, d), lambda bi, i, j: (0, 0)),
            pl.BlockSpec((1, d), lambda bi, i, j: (0, 0)),
        ],
        out_specs=pl.BlockSpec((None, tm, d), lambda bi, i, j: (bi, i, 0)),
        out_shape=jax.ShapeDtypeStruct((b, t, d), F32),
        scratch_shapes=[pltpu.VMEM((nj, tm, tn), F32)],
        compiler_params=_cparams(3),
        name=name,
    )(a, w, x, mod, ln_g, ln_b)


def _rope_tables(t_len):
    rows = t_len // GRID_W
    row = jnp.repeat(jnp.arange(rows, dtype=F32), GRID_W)
    col = jnp.tile(jnp.arange(GRID_W, dtype=F32), rows)
    n_freq = HD // 4
    inv_freq = ROPE_THETA ** (-jnp.arange(n_freq, dtype=F32) / n_freq)
    ang = jnp.concatenate([row[:, None] * inv_freq, col[:, None] * inv_freq], -1)
    cos, sin = jnp.cos(ang), jnp.sin(ang)
    return jnp.concatenate([cos, cos], -1), jnp.concatenate([-sin, sin], -1)


def kernel(x, c, ctx, c_ctx, w_ada, b_ada, w_in, hgrn_lb, hgrn_norm, pool_w, pool_scale, attn_sink, w_branch,
           w_out, ln1_g, ln1_b, w_ff1, w_ff2, ln2_g, ln2_b):
    depth = w_ada.shape[0]
    bsz, t_len, d = x.shape
    tc_len = ctx.shape[1]
    assert d == D_MODEL and w_in.shape[2] == N_COLS and bsz <= MOD_ROWS // 2
    alpha = (2.0 * depth) ** 0.25

    lb = jnp.cumsum(jax.nn.softmax(hgrn_lb.astype(F32), axis=0), axis=0)
    lb = lb - lb[0:1]
    gconst = jnp.stack([lb, 1.0 - lb], axis=2)

    cc = jnp.zeros((MOD_ROWS, d), F32).at[:bsz].set(c).at[MOD_ROWS // 2].set(c_ctx)
    mod = _modulation(cc, w_ada, b_ada).reshape(depth, MOD_ROWS, N_ADA, d)
    cos2, sin2 = _rope_tables(t_len)

    n_ctx = bsz * tc_len
    tm_l = min(1024, t_len)
    tm_c = min(1024, n_ctx)
    tm_ff_l = min(512, t_len)
    tm_ff_c = min(512, n_ctx)

    w_br_b, w_out_b = w_branch.astype(BF16), w_out.astype(BF16)
    w_ff2_b, pool_w_b = w_ff2.astype(BF16), pool_w.astype(BF16)

    def flat(a):
        return a.reshape(1, n_ctx, a.shape[-1])

    xc, xl = flat(ctx), x
    for l in range(depth):
        need_ctx = l < depth - 1
        mod_l = mod[l]
        g1, b1 = ln1_g[l].reshape(1, d), ln1_b[l].reshape(1, d)
        g2, b2 = ln2_g[l].reshape(1, d), ln2_b[l].reshape(1, d)
        sink_l = attn_sink[l].astype(F32)
        scale_l = pool_scale[l].reshape(1, -1)
        norm_l = hgrn_norm[l].reshape(1, -1)

        in_proj = functools.partial(_mod_matmul, sh_idx=0, sc_idx=1, sq_relu=False, out_dtype=F32, tn=768)
        ul = in_proj(xl, mod_l, w_in, l, ctx_row=False, tm=tm_l, name="in_proj_latent")
        uc_flat = in_proj(xc, mod_l, w_in, l, ctx_row=True, tm=tm_c, name="in_proj_ctx")
        uc = uc_flat.reshape(bsz, tc_len, N_COLS)

        ya_c, ya_l = _hgrn_mixer(uc, ul, gconst[l], norm_l, need_ctx)
        yb_l = _pool_mixer(ul, pool_w_b, l, scale_l)
        q_r, kv_r = _rope_qkv(ul, cos2, sin2)
        yc_l = _attention_latent(sink_l, q_r, kv_r, uc)

        def finish_stream(xs, us, ya, yb, yc, ctx_row, tm, tm_ff, l=l, mod_l=mod_l, g1=g1, b1=b1, g2=g2, b2=b2):
            merged = _merge_branches(us, ya, yb, yc, w_br_b, l, tm)
            xs = _matmul_res_ln(merged, w_out_b, l, xs, mod_l, g1, b1, ctx_row=ctx_row, gate_idx=2, alpha=alpha,
                                tm=tm_ff, tn=d, name="out_proj_ln")
            hidden = _mod_matmul(xs, mod_l, w_ff1, l, ctx_row=ctx_row, sh_idx=3, sc_idx=4, sq_relu=True,
                                 out_dtype=BF16, tm=tm, tn=1024, name="ffn_up")
            return _matmul_res_ln(hidden, w_ff2_b, l, xs, mod_l, g2, b2, ctx_row=ctx_row, gate_idx=5,
                                  alpha=alpha, tm=tm_ff, tn=256, name="ffn_down_ln")

        xl = finish_stream(xl, ul, ya_l, yb_l, yc_l, False, tm_l, tm_ff_l)
        if need_ctx:
            yb_c = _pool_mixer(uc, pool_w_b, l, scale_l)
            yc_c = _attention_ctx(sink_l, uc)
            xc = finish_stream(xc, uc_flat, flat(ya_c), flat(yb_c), flat(yc_c), True, tm_c, tm_ff_c)
    return xl
```

```python
import functools

import jax
import jax.numpy as jnp
from jax import lax
from jax.experimental import pallas as pl
from jax.experimental.pallas import tpu as pltpu

F32 = jnp.float32
BF16 = jnp.bfloat16

D_MODEL = 2048
GRID_W = 64
HA, DK_A = 8, 128
WK_A = HA * DK_A
POOL_WINDOWS = (2, 4, 8, 16)
POOL_GW = 256
HQ, HKV, HD = 8, 2, 128
GQ = HQ // HKV
WINDOW = 128
BLOCK_Q = 128
ROPE_THETA = 10000.0
N_BRANCH = 3
BRANCH_W = 1024
N_ADA = 6
LN_EPS = 1e-5
RMS_EPS = 1e-6

COL_AQ = 0
COL_AI = COL_AQ + WK_A
COL_AG = COL_AI + BRANCH_W
COL_AFF = COL_AG + BRANCH_W
COL_AFB = COL_AFF + WK_A
COL_B = COL_AFB + WK_A
COL_CQ = COL_B + BRANCH_W
COL_CK = COL_CQ + BRANCH_W
COL_CV = COL_CK + HKV * HD
COL_GATE = COL_CV + HKV * HD
N_COLS = COL_GATE + N_BRANCH * D_MODEL

LANES = 128
VMEM_LIMIT = 56 * 1024 * 1024

CHUNK = 64
SUB = 16
HGRN_HEADS_PER_STEP = 2
MOD_ROWS = 16


def _cparams(n_axes):
    return pltpu.CompilerParams(dimension_semantics=("arbitrary",) * n_axes, vmem_limit_bytes=VMEM_LIMIT)


def _dot(a, b):
    return jnp.dot(a, b, preferred_element_type=F32)


def _dot_nt(a, b):
    return lax.dot_general(a, b, (((1,), (1,)), ((), ())), preferred_element_type=F32)


def _dot_tn(a, b):
    return lax.dot_general(a, b, (((0,), (0,)), ((), ())), preferred_element_type=F32)


def _mod_kernel(c_ref, w_ref, b_ref, o_ref):
    c = c_ref[...]
    s = (c * jax.nn.sigmoid(c)).astype(BF16)
    o_ref[...] = _dot(s, w_ref[...].astype(BF16)) + b_ref[...]


def _modulation(cc, w_ada, b_ada):
    depth, d, n = w_ada.shape
    tn = 1024
    return pl.pallas_call(
        _mod_kernel,
        grid=(depth, n // tn),
        in_specs=[
            pl.BlockSpec((MOD_ROWS, d), lambda l, j: (0, 0)),
            pl.BlockSpec((None, d, tn), lambda l, j: (l, 0, j)),
            pl.BlockSpec((None, 1, tn), lambda l, j: (l, 0, j)),
        ],
        out_specs=pl.BlockSpec((None, MOD_ROWS, tn), lambda l, j: (l, 0, j)),
        out_shape=jax.ShapeDtypeStruct((depth, MOD_ROWS, n), F32),
        compiler_params=_cparams(2),
        name="adaln_mod",
    )(cc, w_ada, b_ada.reshape(depth, 1, n))


def _modmm_kernel(x_ref, mod_ref, w_ref, o_ref, xm_ref, *, sh_idx, sc_idx, act):
    @pl.when(pl.program_id(2) == 0)
    def _():
        sh = mod_ref[sh_idx:sh_idx + 1, :]
        sc = mod_ref[sc_idx:sc_idx + 1, :]
        xm_ref[...] = (x_ref[...] * (1.0 + sc) + sh).astype(BF16)

    y = _dot(xm_ref[...], w_ref[...])
    if act == "sq_relu":
        y = jnp.square(jnp.maximum(y, 0.0))
    o_ref[...] = y.astype(o_ref.dtype)


def _mod_matmul(x, mod, w, layer, *, ctx_row, sh_idx, sc_idx, act, out_dtype, tm, tn, name, col0=0, n=None):
    b, t, d = x.shape
    n = w.shape[2] - col0 if n is None else n
    assert col0 % tn == 0 and n % tn == 0
    row = (lambda bi: MOD_ROWS // 2) if ctx_row else (lambda bi: bi)
    return pl.pallas_call(
        functools.partial(_modmm_kernel, sh_idx=sh_idx, sc_idx=sc_idx, act=act),
        grid=(b, t // tm, n // tn),
        in_specs=[
            pl.BlockSpec((None, tm, d), lambda bi, i, j: (bi, i, 0)),
            pl.BlockSpec((None, N_ADA, d), lambda bi, i, j: (row(bi), 0, 0)),
            pl.BlockSpec((None, d, tn), lambda bi, i, j: (layer, 0, col0 // tn + j)),
        ],
        out_specs=pl.BlockSpec((None, tm, tn), lambda bi, i, j: (bi, i, j)),
        out_shape=jax.ShapeDtypeStruct((b, t, n), out_dtype),
        scratch_shapes=[pltpu.VMEM((tm, d), BF16)],
        compiler_params=_cparams(3),
        name=name,
    )(x, mod, w)


def _rope_kernel(q_ref, kv_ref, cos_ref, sin_ref, qo_ref, kvo_ref):
    cos = cos_ref[...]
    sin = sin_ref[...]

    def rope(v):
        return v * cos + pltpu.roll(v, HD // 2, 1) * sin

    for h in range(HQ):
        sl = slice(h * HD, (h + 1) * HD)
        qo_ref[:, sl] = rope(q_ref[:, sl]).astype(BF16)
    for h in range(HKV):
        sl = slice(h * HD, (h + 1) * HD)
        kvo_ref[:, sl] = rope(kv_ref[:, sl]).astype(BF16)
    kvo_ref[:, HKV * HD:] = kv_ref[:, HKV * HD:].astype(BF16)


def _rope_qkv(u, cos2, sin2):
    b, t, _ = u.shape
    tq = min(512, t)
    wq, wkv = HQ * HD, 2 * HKV * HD
    return pl.pallas_call(
        _rope_kernel,
        grid=(b, t // tq),
        in_specs=[
            pl.BlockSpec((None, tq, wq), lambda bi, i: (bi, i, COL_CQ // wq)),
            pl.BlockSpec((None, tq, wkv), lambda bi, i: (bi, i, COL_CK // wkv)),
            pl.BlockSpec((tq, HD), lambda bi, i: (i, 0)),
            pl.BlockSpec((tq, HD), lambda bi, i: (i, 0)),
        ],
        out_specs=[
            pl.BlockSpec((None, tq, wq), lambda bi, i: (bi, i, 0)),
            pl.BlockSpec((None, tq, wkv), lambda bi, i: (bi, i, 0)),
        ],
        out_shape=[jax.ShapeDtypeStruct((b, t, wq), BF16), jax.ShapeDtypeStruct((b, t, wkv), BF16)],
        compiler_params=_cparams(2),
        name="rope_qkv",
    )(u, u, cos2, sin2)


MASKED = -1e30


def _softmax_pv_heads(s, bias, sink_ref, kv, vals, o_ref):
    rows = s.shape[0] // GQ
    for g in range(GQ):
        sg = s[g * rows:(g + 1) * rows] * (HD ** -0.5)
        if bias is not None:
            sg = sg + bias
        sink = sink_ref[kv * GQ + g]
        m = jnp.maximum(jnp.max(sg, axis=-1, keepdims=True), sink)
        e = jnp.exp(sg - m)
        denom = jnp.sum(e, axis=-1, keepdims=True) + jnp.exp(sink - m)
        o_ref[:, g * HD:(g + 1) * HD] = (_dot(e.astype(BF16), vals) * (1.0 / denom)).astype(o_ref.dtype)


def _attn_latent_kernel(sink_ref, bias_ref, q_ref, kl_ref, km_ref, kr_ref, vl_ref, vm_ref, vr_ref, kc_ref,
                        vc_ref, o_ref):
    kv = pl.program_id(1)
    q = jnp.concatenate([q_ref[:, g * HD:(g + 1) * HD] for g in range(GQ)], axis=0)
    keys = jnp.concatenate([kl_ref[...], km_ref[...], kr_ref[...], kc_ref[...].astype(BF16)], axis=0)
    vals = jnp.concatenate([vl_ref[...], vm_ref[...], vr_ref[...], vc_ref[...].astype(BF16)], axis=0)
    _softmax_pv_heads(_dot_nt(q, keys), bias_ref[...], sink_ref, kv, vals, o_ref)


def _band_bias(t_len, tc):
    bq = BLOCK_Q
    r = jnp.arange(bq)[:, None]
    c = jnp.arange(3 * bq)[None, :]
    in_band = jnp.abs(r - (c - bq)) <= WINDOW
    kinds = []
    for first, last in ((False, False), (True, False), (False, True), (True, True)):
        ok = in_band
        if first:
            ok = ok & (c >= bq)
        if last:
            ok = ok & (c < 2 * bq)
        kinds.append(jnp.concatenate([jnp.where(ok, 0.0, MASKED), jnp.zeros((bq, tc))], axis=1))
    return jnp.stack(kinds).astype(F32)


def _attention_latent(sink, q_r, kv_r, uc):
    b, t, _ = q_r.shape
    tc = uc.shape[1]
    nb = t // BLOCK_Q
    wq = GQ * HD
    nkeys = 3 * BLOCK_Q + tc
    kblk = lambda off: pl.BlockSpec(
        (None, BLOCK_Q, HD), lambda bi, kv, n: (bi, jnp.clip(n + off, 0, nb - 1), kv))
    vblk = lambda off: pl.BlockSpec(
        (None, BLOCK_Q, HD), lambda bi, kv, n: (bi, jnp.clip(n + off, 0, nb - 1), HKV + kv))
    kind = lambda n: (n == 0).astype(jnp.int32) + 2 * (n == nb - 1).astype(jnp.int32)
    return pl.pallas_call(
        _attn_latent_kernel,
        grid=(b, HKV, nb),
        in_specs=[
            pl.BlockSpec(memory_space=pltpu.SMEM),
            pl.BlockSpec((None, BLOCK_Q, nkeys), lambda bi, kv, n: (kind(n), 0, 0)),
            pl.BlockSpec((None, BLOCK_Q, wq), lambda bi, kv, n: (bi, n, kv)),
            kblk(-1), kblk(0), kblk(1), vblk(-1), vblk(0), vblk(1),
            pl.BlockSpec((None, tc, HD), lambda bi, kv, n: (bi, 0, COL_CK // HD + kv)),
            pl.BlockSpec((None, tc, HD), lambda bi, kv, n: (bi, 0, COL_CV // HD + kv)),
        ],
        out_specs=pl.BlockSpec((None, BLOCK_Q, wq), lambda bi, kv, n: (bi, n, kv)),
        out_shape=jax.ShapeDtypeStruct((b, t, HQ * HD), BF16),
        compiler_params=_cparams(3),
        name="attn_latent",
    )(sink, _band_bias(t, tc), q_r, kv_r, kv_r, kv_r, kv_r, kv_r, kv_r, uc, uc)


def _attn_ctx_kernel(sink_ref, q_ref, kc_ref, vc_ref, o_ref):
    kv = pl.program_id(1)
    q = jnp.concatenate([q_ref[:, g * HD:(g + 1) * HD] for g in range(GQ)], axis=0).astype(BF16)
    s = _dot_nt(q, kc_ref[...].astype(BF16))
    _softmax_pv_heads(s, None, sink_ref, kv, vc_ref[...].astype(BF16), o_ref)


def _attention_ctx(sink, uc):
    b, tc, _ = uc.shape
    wq = GQ * HD
    return pl.pallas_call(
        _attn_ctx_kernel,
        grid=(b, HKV),
        in_specs=[
            pl.BlockSpec(memory_space=pltpu.SMEM),
            pl.BlockSpec((None, tc, wq), lambda bi, kv: (bi, 0, COL_CQ // wq + kv)),
            pl.BlockSpec((None, tc, HD), lambda bi, kv: (bi, 0, COL_CK // HD + kv)),
            pl.BlockSpec((None, tc, HD), lambda bi, kv: (bi, 0, COL_CV // HD + kv)),
        ],
        out_specs=pl.BlockSpec((None, tc, wq), lambda bi, kv: (bi, 0, kv)),
        out_shape=jax.ShapeDtypeStruct((b, tc, HQ * HD), BF16),
        compiler_params=_cparams(2),
        name="attn_ctx",
    )(sink, uc, uc, uc)


def _pool_kernel(u_ref, w_ref, sc_ref, o_ref):
    t_len = u_ref.shape[0]
    grp = pl.program_id(1)
    x = u_ref[...]
    t = lax.broadcasted_iota(jnp.int32, x.shape, 0)

    def shift_down(a, k):
        return jnp.where(t >= k, pltpu.roll(a, k, 0), 0.0)

    def shift_up(a, k):
        return jnp.where(t < t_len - k, pltpu.roll(a, t_len - k, 0), 0.0)

    for j, win in enumerate(POOL_WINDOWS):
        @pl.when(grp == j)
        def _(win=win):
            half = win // 2
            trail, lead, step = x, x, 1
            while step < half:
                trail = trail + shift_down(trail, step)
                lead = lead + shift_up(lead, step)
                step *= 2
            total = shift_down(trail, 1) + lead
            cnt = (jnp.minimum(t + half, t_len) - jnp.maximum(t - half, 0)).astype(F32)
            y = _dot((total / cnt - x).astype(BF16), w_ref[...])
            o_ref[...] = (y * sc_ref[...]).astype(o_ref.dtype)


def _pool_mixer(u, pool_w, layer, pool_scale):
    b, t, _ = u.shape
    ng = len(POOL_WINDOWS)
    return pl.pallas_call(
        _pool_kernel,
        grid=(b, ng),
        in_specs=[
            pl.BlockSpec((None, t, POOL_GW), lambda bi, g: (bi, 0, COL_B // POOL_GW + g)),
            pl.BlockSpec((None, None, POOL_GW, POOL_GW), lambda bi, g: (layer, g, 0, 0)),
            pl.BlockSpec((1, POOL_GW), lambda bi, g: (0, g)),
        ],
        out_specs=pl.BlockSpec((None, t, POOL_GW), lambda bi, g: (bi, 0, g)),
        out_shape=jax.ShapeDtypeStruct((b, t, ng * POOL_GW), BF16),
        compiler_params=_cparams(2),
        name="pool_mixer",
    )(u, pool_w, pool_scale)


SAFE_LOG_DECAY = 80.0
F32_TINY = 1.1754944e-38
DIAG_CODE, OFF_CODE = 2.0, 1.0


def _hgrn_pair_codes(tb, rev):
    r = jnp.arange(tb)[:, None]
    c = jnp.arange(tb)[None, :]
    ordered = (c >= r) if rev else (c <= r)
    same_chunk = (r // CHUNK) == (c // CHUNK)
    same_sub = (r // SUB) == (c // SUB)
    return jnp.where(same_sub & ordered, DIAG_CODE, jnp.where(same_chunk & ordered, OFF_CODE, 0.0)).astype(F32)


def _hgrn_kernel(*refs, rev, final, has_init, tb):
    refs = list(refs)
    q_ref, i_ref, f_ref, gconst_ref, code_ref = refs[:5]
    del refs[:5]
    s0_ref = refs.pop(0) if has_init else None
    if final:
        g_ref, ob_ref, ng_ref = refs[:3]
        del refs[:3]
    o_ref, s_out_ref, st_ref, k_ref, bsub_ref, rest_ref = refs
    step = pl.program_id(2)
    n_chunks = tb // CHUNK
    n_sub = CHUNK // SUB
    heads = st_ref.shape[0]

    @pl.when(step == 0)
    def _():
        st_ref[...] = s0_ref[...] if has_init else jnp.zeros_like(st_ref)

    code = code_ref[...]
    row = lax.broadcasted_iota(jnp.int32, (tb, LANES), 0)
    pos_in_sub = row % SUB
    sub_of_row = (row % CHUNK) // SUB
    chunk_of_row = row // CHUNK
    sub_order = list(reversed(range(n_sub))) if rev else list(range(n_sub))
    chunk_order = list(reversed(range(n_chunks))) if rev else list(range(n_chunks))

    def rows_of(values, height):
        return jnp.concatenate([jnp.broadcast_to(val, (height, LANES)) for val in values], axis=0)

    def diag_exact(q, k, v, b_sub):
        pos = pos_in_sub
        acc = jnp.zeros((tb, LANES), F32)
        for d in range(SUB):
            if d == 0:
                kd, bd, vd = k, b_sub, v
            else:
                sh = tb - d if rev else d
                kd, bd, vd = pltpu.roll(k, sh, 0), pltpu.roll(b_sub, sh, 0), pltpu.roll(v, sh, 0)
            a = jnp.sum(q * kd * jnp.exp(jnp.minimum(b_sub - bd, 0.0)), axis=-1, keepdims=True)
            valid = (pos <= SUB - 1 - d) if rev else (pos >= d)
            acc = acc + jnp.where(valid, a, 0.0) * vd
        return acc

    def finish(o, cols):
        if final:
            o = o + ob_ref[:, cols]
            g = g_ref[:, cols]
            o = o * lax.rsqrt(jnp.mean(jnp.square(o), axis=-1, keepdims=True) + RMS_EPS)
            o = o * ng_ref[:, cols] * (g * jax.nn.sigmoid(g))
        return o.astype(o_ref.dtype)

    worst = jnp.zeros((), F32)
    for hh in range(heads):
        cols = slice(hh * LANES, (hh + 1) * LANES)
        q = q_ref[:, cols]
        v = i_ref[:, cols]
        z = f_ref[:, cols]
        lb = gconst_ref[0:1, cols]
        one_m_lb = gconst_ref[1:2, cols]

        e = jnp.exp(-jnp.abs(z))
        r = 1.0 / (1.0 + e)
        num = jnp.where(z >= 0.0, 1.0 + lb * e, lb + e)
        logf = jnp.log(jnp.maximum(num * r, F32_TINY))
        k = one_m_lb * jnp.where(z >= 0.0, e * r, r)
        v16 = v.astype(BF16)

        b_sub = logf
        for s in (1, 2, 4, 8):
            if rev:
                b_sub = b_sub + jnp.where(pos_in_sub < SUB - s, pltpu.roll(b_sub, tb - s, 0), 0.0)
            else:
                b_sub = b_sub + jnp.where(pos_in_sub >= s, pltpu.roll(b_sub, s, 0), 0.0)
        last = 0 if rev else SUB - 1
        zero_row = jnp.zeros((1, LANES), F32)
        before_sub = [None] * (n_chunks * n_sub)
        before_chunk = [None] * n_chunks
        run_block = zero_row
        for c in chunk_order:
            run = zero_row
            for j in sub_order:
                u = c * n_sub + j
                before_sub[u] = run
                run = run + b_sub[u * SUB + last:u * SUB + last + 1]
            before_chunk[c] = run_block
            run_block = run_block + run
        b_chunk = b_sub + rows_of(before_sub, SUB)
        b_blk = b_chunk + rows_of(before_chunk, CHUNK)

        q_sub = q * jnp.exp(b_sub)
        q_parts, k_parts = [], []
        for i in range(1, n_sub):
            j = sub_order[i]
            b_ref = rows_of([before_sub[c * n_sub + j] for c in range(n_chunks)], CHUNK)
            is_src = (sub_of_row > j) if rev else (sub_of_row < j)
            q_parts.append(jnp.where(sub_of_row == j, q_sub, 0.0).astype(BF16))
            k_parts.append(jnp.where(is_src, k * jnp.exp(jnp.minimum(b_ref - b_chunk, 0.0)), 0.0).astype(BF16))
        s_off = _dot_nt(jnp.concatenate(q_parts, axis=1), jnp.concatenate(k_parts, axis=1))
        a_off = jnp.where(code == OFF_CODE, s_off, 0.0)

        if n_chunks > 1:
            q_chunk = q * jnp.exp(b_chunk)
            q_parts, k_parts = [], []
            for i in range(1, n_chunks):
                c = chunk_order[i]
                is_src = (chunk_of_row > c) if rev else (chunk_of_row < c)
                q_parts.append(jnp.where(chunk_of_row == c, q_chunk, 0.0).astype(BF16))
                k_parts.append(
                    jnp.where(is_src, k * jnp.exp(jnp.minimum(before_chunk[c] - b_blk, 0.0)), 0.0).astype(BF16))
            a_off = a_off + _dot_nt(jnp.concatenate(q_parts, axis=1), jnp.concatenate(k_parts, axis=1))
        o = _dot(a_off.astype(BF16), v16)

        st = st_ref[hh]
        o = o + _dot_nt((q * jnp.exp(b_blk)).astype(BF16), st.astype(BF16))
        k_out = (k * jnp.exp(run_block - b_blk)).astype(BF16)
        st_ref[hh] = st * jnp.exp(run_block) + _dot_tn(v16, k_out)

        k_ref[:, cols] = k
        bsub_ref[:, cols] = b_sub
        rest_ref[:, cols] = o

        k_sub = (k * jnp.exp(-b_sub)).astype(BF16)
        s_diag = _dot_nt(q_sub.astype(BF16), k_sub)
        o = o + _dot(jnp.where(code == DIAG_CODE, s_diag, 0.0).astype(BF16), v16)
        worst = jnp.maximum(worst, jnp.max(-b_sub))
        o_ref[:, cols] = finish(o, cols)

    @pl.when(worst > SAFE_LOG_DECAY)
    def _():
        for hh in range(heads):
            cols = slice(hh * LANES, (hh + 1) * LANES)
            exact = diag_exact(q_ref[:, cols], k_ref[:, cols], i_ref[:, cols], bsub_ref[:, cols])
            o_ref[:, cols] = finish(rest_ref[:, cols] + exact, cols)

    @pl.when(step == pl.num_programs(2) - 1)
    def _():
        s_out_ref[...] = st_ref[...]


def _hgrn_scan(u, gconst, rev, init_state=None, final_inputs=None):
    b, t, _ = u.shape
    tb = min(256, t)
    nblk = t // tb
    final = final_inputs is not None
    has_init = init_state is not None
    code = _hgrn_pair_codes(tb, rev)
    heads = HGRN_HEADS_PER_STEP
    hw = heads * LANES
    row = (lambda s: nblk - 1 - s) if rev else (lambda s: s)

    def col_spec(col0):
        return pl.BlockSpec((None, tb, hw), lambda bi, h, s: (bi, row(s), col0 // hw + h))

    state_spec = pl.BlockSpec((None, heads, LANES, LANES), lambda bi, h, s: (bi, h, 0, 0))
    in_specs = [col_spec(COL_AQ), col_spec(COL_AI), col_spec(COL_AFB if rev else COL_AFF),
                pl.BlockSpec((2, hw), lambda bi, h, s: (0, h)),
                pl.BlockSpec((tb, tb), lambda bi, h, s: (0, 0))]
    args = [u, u, u, gconst, code]
    if has_init:
        in_specs.append(state_spec)
        args.append(init_state)
    if final:
        other, norm_g = final_inputs
        in_specs += [col_spec(COL_AG), col_spec(0), pl.BlockSpec((1, hw), lambda bi, h, s: (0, h))]
        args += [u, other, norm_g]
    return pl.pallas_call(
        functools.partial(_hgrn_kernel, rev=rev, final=final, has_init=has_init, tb=tb),
        grid=(b, HA // heads, nblk),
        in_specs=in_specs,
        out_specs=[col_spec(0), state_spec],
        out_shape=[jax.ShapeDtypeStruct((b, t, BRANCH_W), BF16 if final else F32),
                   jax.ShapeDtypeStruct((b, HA, LANES, LANES), F32)],
        scratch_shapes=[pltpu.VMEM((heads, LANES, LANES), F32),
                        pltpu.VMEM((tb, hw), F32), pltpu.VMEM((tb, hw), F32), pltpu.VMEM((tb, hw), F32)],
        compiler_params=_cparams(3),
        name="hgrn_fwd" if final else "hgrn_bwd",
    )(*args)


def _hgrn_mixer(uc, ul, gconst, norm_g, need_ctx):
    ob_c, sb = _hgrn_scan(uc, gconst[1], rev=True)
    ob_l, _ = _hgrn_scan(ul, gconst[1], rev=True, init_state=sb)
    if need_ctx:
        ya_c, sf = _hgrn_scan(uc, gconst[0], rev=False, final_inputs=(ob_c, norm_g))
    else:
        ya_c = None
        _, sf = _hgrn_scan(uc, gconst[0], rev=False)
    ya_l, _ = _hgrn_scan(ul, gconst[0], rev=False, init_state=sf, final_inputs=(ob_l, norm_g))
    return ya_c, ya_l


def _merge_kernel(ya_ref, yb_ref, yc_ref, wb_ref, g0_ref, g1_ref, g2_ref, o_ref):
    acc = jax.nn.sigmoid(g0_ref[...].astype(F32)) * _dot(ya_ref[...], wb_ref[0])
    acc = acc + jax.nn.sigmoid(g1_ref[...].astype(F32)) * _dot(yb_ref[...], wb_ref[1])
    acc = acc + jax.nn.sigmoid(g2_ref[...].astype(F32)) * _dot(yc_ref[...], wb_ref[2])
    o_ref[...] = acc.astype(o_ref.dtype)


def _merge_branches(gates, ya, yb, yc, w_branch, layer, tm):
    b, t, _ = gates.shape
    d = w_branch.shape[3]
    tn = 512
    y_spec = pl.BlockSpec((None, tm, BRANCH_W), lambda bi, i, j: (bi, i, 0))
    gate = lambda br: pl.BlockSpec(
        (None, tm, tn), lambda bi, i, j: (bi, i, (br * d) // tn + j))
    return pl.pallas_call(
        _merge_kernel,
        grid=(b, t // tm, d // tn),
        in_specs=[y_spec, y_spec, y_spec,
                  pl.BlockSpec((None, N_BRANCH, BRANCH_W, tn), lambda bi, i, j: (layer, 0, 0, j)),
                  gate(0), gate(1), gate(2)],
        out_specs=pl.BlockSpec((None, tm, tn), lambda bi, i, j: (bi, i, j)),
        out_shape=jax.ShapeDtypeStruct((b, t, d), BF16),
        compiler_params=_cparams(3),
        name="merge_branches",
    )(ya, yb, yc, w_branch, gates, gates, gates)


def _mmln_kernel(a_ref, w_ref, x_ref, mod_ref, g_ref, b_ref, o_ref, y_ref, *, gate_idx, nj, tn, alpha):
    if nj == 1:
        half = a_ref.shape[0] // 2
        for rows in (slice(0, half), slice(half, 2 * half)):
            res = alpha * x_ref[rows, :] + mod_ref[gate_idx:gate_idx + 1, :] * _dot(a_ref[rows, :], w_ref[...])
            cen = res - jnp.mean(res, axis=-1, keepdims=True)
            var = jnp.mean(jnp.square(cen), axis=-1, keepdims=True)
            o_ref[rows, :] = cen * lax.rsqrt(var + LN_EPS) * g_ref[...] + b_ref[...]
        return

    j = pl.program_id(2)
    y_ref[j] = _dot(a_ref[...], w_ref[...])

    @pl.when(j == nj - 1)
    def _():
        d = nj * tn
        total = jnp.zeros((a_ref.shape[0], 1), F32)
        for jj in range(nj):
            cols = slice(jj * tn, (jj + 1) * tn)
            res = alpha * x_ref[:, cols] + mod_ref[gate_idx:gate_idx + 1, cols] * y_ref[jj]
            y_ref[jj] = res
            total = total + jnp.sum(res, axis=-1, keepdims=True)
        mu = total / d
        sq = jnp.zeros_like(total)
        for jj in range(nj):
            sq = sq + jnp.sum(jnp.square(y_ref[jj] - mu), axis=-1, keepdims=True)
        inv = lax.rsqrt(sq / d + LN_EPS)
        for jj in range(nj):
            cols = slice(jj * tn, (jj + 1) * tn)
            o_ref[:, cols] = (y_ref[jj] - mu) * inv * g_ref[:, cols] + b_ref[:, cols]


def _matmul_res_ln(a, w, layer, x, mod, ln_g, ln_b, *, ctx_row, gate_idx, alpha, tm, tn, name):
    b, t, kdim = a.shape
    d = w.shape[2]
    nj = d // tn
    row = (lambda bi: MOD_ROWS // 2) if ctx_row else (lambda bi: bi)
    return pl.pallas_call(
        functools.partial(_mmln_kernel, gate_idx=gate_idx, nj=nj, tn=tn, alpha=alpha),
        grid=(b, t // tm, nj),
        in_specs=[
            pl.BlockSpec((None, tm, kdim), lambda bi, i, j: (bi, i, 0)),
            pl.BlockSpec((None, kdim, tn), lambda bi, i, j: (layer, 0, j)),
            pl.BlockSpec((None, tm, d), lambda bi, i, j: (bi, i, 0)),
            pl.BlockSpec((None, N_ADA, d), lambda bi, i, j: (row(bi), 0, 0)),
            pl.BlockSpec((1, d), lambda bi, i, j: (0, 0)),
            pl.BlockSpec((1, d), lambda bi, i, j: (0, 0)),
        ],
        out_specs=pl.BlockSpec((None, tm, d), lambda bi, i, j: (bi, i, 0)),
        out_shape=jax.ShapeDtypeStruct((b, t, d), F32),
        scratch_shapes=[pltpu.VMEM((nj, tm, tn), F32)],
        compiler_params=_cparams(3),
        name=name,
    )(a, w, x, mod, ln_g, ln_b)


def _rope_tables(t_len):
    rows = t_len // GRID_W
    row = jnp.repeat(jnp.arange(rows, dtype=F32), GRID_W)
    col = jnp.tile(jnp.arange(GRID_W, dtype=F32), rows)
    n_freq = HD // 4
    inv_freq = ROPE_THETA ** (-jnp.arange(n_freq, dtype=F32) / n_freq)
    ang = jnp.concatenate([row[:, None] * inv_freq, col[:, None] * inv_freq], -1)
    cos, sin = jnp.cos(ang), jnp.sin(ang)
    return jnp.concatenate([cos, cos], -1), jnp.concatenate([-sin, sin], -1)


def kernel(x, c, ctx, c_ctx, w_ada, b_ada, w_in, hgrn_lb, hgrn_norm, pool_w, pool_scale, attn_sink, w_branch,
           w_out, ln1_g, ln1_b, w_ff1, w_ff2, ln2_g, ln2_b):
    depth = w_ada.shape[0]
    bsz, t_len, d = x.shape
    tc_len = ctx.shape[1]
    assert d == D_MODEL and w_in.shape[2] == N_COLS and bsz <= MOD_ROWS // 2
    alpha = (2.0 * depth) ** 0.25

    lb = jnp.cumsum(jax.nn.softmax(hgrn_lb.astype(F32), axis=0), axis=0)
    lb = lb - lb[0:1]
    gconst = jnp.stack([lb, 1.0 - lb], axis=2)

    cc = jnp.zeros((MOD_ROWS, d), F32).at[:bsz].set(c).at[MOD_ROWS // 2].set(c_ctx)
    mod = _modulation(cc, w_ada, b_ada).reshape(depth, MOD_ROWS, N_ADA, d)
    cos2, sin2 = _rope_tables(t_len)

    n_ctx = bsz * tc_len
    tm_l = min(1024, t_len)
    tm_c = min(1024, n_ctx)
    tm_ff_l = min(512, t_len)
    tm_ff_c = min(512, n_ctx)

    w_in_b, w_br_b, w_out_b = w_in.astype(BF16), w_branch.astype(BF16), w_out.astype(BF16)
    w_ff1_b, w_ff2_b, pool_w_b = w_ff1.astype(BF16), w_ff2.astype(BF16), pool_w.astype(BF16)

    def flat(a):
        return a.reshape(1, n_ctx, a.shape[-1])

    xc, xl = flat(ctx), x
    for l in range(depth):
        need_ctx = l < depth - 1
        mod_l = mod[l]
        g1, b1 = ln1_g[l].reshape(1, d), ln1_b[l].reshape(1, d)
        g2, b2 = ln2_g[l].reshape(1, d), ln2_b[l].reshape(1, d)
        sink_l = attn_sink[l].astype(F32)
        scale_l = pool_scale[l].reshape(1, -1)
        norm_l = hgrn_norm[l].reshape(1, -1)

        in_proj = functools.partial(_mod_matmul, sh_idx=0, sc_idx=1, act=None, out_dtype=F32, tn=768, n=COL_GATE)
        gate_proj = functools.partial(_mod_matmul, sh_idx=0, sc_idx=1, act=None, out_dtype=BF16, tn=768,
                                      col0=COL_GATE)
        ul = in_proj(xl, mod_l, w_in_b, l, ctx_row=False, tm=tm_l, name="in_proj_latent")
        gates_l = gate_proj(xl, mod_l, w_in_b, l, ctx_row=False, tm=tm_l, name="gate_proj_latent")
        uc_flat = in_proj(xc, mod_l, w_in_b, l, ctx_row=True, tm=tm_c, name="in_proj_ctx")
        uc = uc_flat.reshape(bsz, tc_len, COL_GATE)

        ya_c, ya_l = _hgrn_mixer(uc, ul, gconst[l], norm_l, need_ctx)
        yb_l = _pool_mixer(ul, pool_w_b, l, scale_l)
        q_r, kv_r = _rope_qkv(ul, cos2, sin2)
        yc_l = _attention_latent(sink_l, q_r, kv_r, uc)

        def finish_stream(xs, gates, ya, yb, yc, ctx_row, tm, tm_ff, l=l, mod_l=mod_l, g1=g1, b1=b1, g2=g2,
                          b2=b2):
            merged = _merge_branches(gates, ya, yb, yc, w_br_b, l, tm)
            xs = _matmul_res_ln(merged, w_out_b, l, xs, mod_l, g1, b1, ctx_row=ctx_row, gate_idx=2, alpha=alpha,
                                tm=tm_ff, tn=d, name="out_proj_ln")
            hidden = _mod_matmul(xs, mod_l, w_ff1_b, l, ctx_row=ctx_row, sh_idx=3, sc_idx=4, act="sq_relu",
                                 out_dtype=BF16, tm=tm, tn=1024, name="ffn_up")
            return _matmul_res_ln(hidden, w_ff2_b, l, xs, mod_l, g2, b2, ctx_row=ctx_row, gate_idx=5,
                                  alpha=alpha, tm=tm_ff, tn=256, name="ffn_down_ln")

        xl = finish_stream(xl, gates_l, ya_l, yb_l, yc_l, False, tm_l, tm_ff_l)
        if need_ctx:
            yb_c = _pool_mixer(uc, pool_w_b, l, scale_l)
            yc_c = _attention_ctx(sink_l, uc)
            gates_c = gate_proj(xc, mod_l, w_in_b, l, ctx_row=True, tm=tm_c, name="gate_proj_ctx")
            xc = finish_stream(xc, gates_c, flat(ya_c), flat(yb_c), flat(yc_c), True, tm_c, tm_ff_c)
    return xl
```

```python
import functools

import jax
import jax.numpy as jnp
from jax import lax
from jax.experimental import pallas as pl
from jax.experimental.pallas import tpu as pltpu

F32 = jnp.float32
BF16 = jnp.bfloat16

D_MODEL = 2048
GRID_W = 64
HA, DK_A = 8, 128
WK_A = HA * DK_A
POOL_WINDOWS = (2, 4, 8, 16)
POOL_GW = 256
HQ, HKV, HD = 8, 2, 128
GQ = HQ // HKV
WINDOW = 128
BLOCK_Q = 128
ROPE_THETA = 10000.0
N_BRANCH = 3
BRANCH_W = 1024
N_ADA = 6
LN_EPS = 1e-5
RMS_EPS = 1e-6

COL_AQ = 0
COL_AI = COL_AQ + WK_A
COL_AG = COL_AI + BRANCH_W
COL_AFF = COL_AG + BRANCH_W
COL_AFB = COL_AFF + WK_A
COL_B = COL_AFB + WK_A
COL_CQ = COL_B + BRANCH_W
COL_CK = COL_CQ + BRANCH_W
COL_CV = COL_CK + HKV * HD
COL_GATE = COL_CV + HKV * HD
N_COLS = COL_GATE + N_BRANCH * D_MODEL

LANES = 128
VMEM_LIMIT = 56 * 1024 * 1024

CHUNK = 64
SUB = 16
HGRN_HEADS_PER_STEP = 8
MOD_ROWS = 16


def _cparams(n_axes):
    return pltpu.CompilerParams(dimension_semantics=("arbitrary",) * n_axes, vmem_limit_bytes=VMEM_LIMIT)


def _dot(a, b):
    return jnp.dot(a, b, preferred_element_type=F32)


def _dot_nt(a, b):
    return lax.dot_general(a, b, (((1,), (1,)), ((), ())), preferred_element_type=F32)


def _dot_tn(a, b):
    return lax.dot_general(a, b, (((0,), (0,)), ((), ())), preferred_element_type=F32)


def _mod_kernel(c_ref, w_ref, b_ref, o_ref):
    c = c_ref[...]
    s = (c * jax.nn.sigmoid(c)).astype(BF16)
    o_ref[...] = _dot(s, w_ref[...].astype(BF16)) + b_ref[...]


def _modulation(cc, w_ada, b_ada):
    depth, d, n = w_ada.shape
    tn = 1024
    return pl.pallas_call(
        _mod_kernel,
        grid=(depth, n // tn),
        in_specs=[
            pl.BlockSpec((MOD_ROWS, d), lambda l, j: (0, 0)),
            pl.BlockSpec((None, d, tn), lambda l, j: (l, 0, j)),
            pl.BlockSpec((None, 1, tn), lambda l, j: (l, 0, j)),
        ],
        out_specs=pl.BlockSpec((None, MOD_ROWS, tn), lambda l, j: (l, 0, j)),
        out_shape=jax.ShapeDtypeStruct((depth, MOD_ROWS, n), F32),
        compiler_params=_cparams(2),
        name="adaln_mod",
    )(cc, w_ada, b_ada.reshape(depth, 1, n))


def _modmm_kernel(x_ref, mod_ref, w_ref, o_ref, xm_ref, *, sh_idx, sc_idx, act):
    @pl.when(pl.program_id(2) == 0)
    def _():
        sh = mod_ref[sh_idx:sh_idx + 1, :]
        sc = mod_ref[sc_idx:sc_idx + 1, :]
        xm_ref[...] = (x_ref[...] * (1.0 + sc) + sh).astype(BF16)

    y = _dot(xm_ref[...], w_ref[...])
    if act == "sq_relu":
        y = jnp.square(jnp.maximum(y, 0.0))
    o_ref[...] = y.astype(o_ref.dtype)


def _mod_matmul(x, mod, w, layer, *, ctx_row, sh_idx, sc_idx, act, out_dtype, tm, tn, name, col0=0, n=None):
    b, t, d = x.shape
    n = w.shape[2] - col0 if n is None else n
    assert col0 % tn == 0 and n % tn == 0
    row = (lambda bi: MOD_ROWS // 2) if ctx_row else (lambda bi: bi)
    return pl.pallas_call(
        functools.partial(_modmm_kernel, sh_idx=sh_idx, sc_idx=sc_idx, act=act),
        grid=(b, t // tm, n // tn),
        in_specs=[
            pl.BlockSpec((None, tm, d), lambda bi, i, j: (bi, i, 0)),
            pl.BlockSpec((None, N_ADA, d), lambda bi, i, j: (row(bi), 0, 0)),
            pl.BlockSpec((None, d, tn), lambda bi, i, j: (layer, 0, col0 // tn + j)),
        ],
        out_specs=pl.BlockSpec((None, tm, tn), lambda bi, i, j: (bi, i, j)),
        out_shape=jax.ShapeDtypeStruct((b, t, n), out_dtype),
        scratch_shapes=[pltpu.VMEM((tm, d), BF16)],
        compiler_params=_cparams(3),
        name=name,
    )(x, mod, w)


def _rope_kernel(q_ref, kv_ref, cos_ref, sin_ref, qo_ref, kvo_ref):
    cos = cos_ref[...]
    sin = sin_ref[...]

    def rope(v):
        return v * cos + pltpu.roll(v, HD // 2, 1) * sin

    for h in range(HQ):
        sl = slice(h * HD, (h + 1) * HD)
        qo_ref[:, sl] = rope(q_ref[:, sl]).astype(BF16)
    for h in range(HKV):
        sl = slice(h * HD, (h + 1) * HD)
        kvo_ref[:, sl] = rope(kv_ref[:, sl]).astype(BF16)
    kvo_ref[:, HKV * HD:] = kv_ref[:, HKV * HD:].astype(BF16)


def _rope_qkv(u, cos2, sin2):
    b, t, _ = u.shape
    tq = min(512, t)
    wq, wkv = HQ * HD, 2 * HKV * HD
    return pl.pallas_call(
        _rope_kernel,
        grid=(b, t // tq),
        in_specs=[
            pl.BlockSpec((None, tq, wq), lambda bi, i: (bi, i, COL_CQ // wq)),
            pl.BlockSpec((None, tq, wkv), lambda bi, i: (bi, i, COL_CK // wkv)),
            pl.BlockSpec((tq, HD), lambda bi, i: (i, 0)),
            pl.BlockSpec((tq, HD), lambda bi, i: (i, 0)),
        ],
        out_specs=[
            pl.BlockSpec((None, tq, wq), lambda bi, i: (bi, i, 0)),
            pl.BlockSpec((None, tq, wkv), lambda bi, i: (bi, i, 0)),
        ],
        out_shape=[jax.ShapeDtypeStruct((b, t, wq), BF16), jax.ShapeDtypeStruct((b, t, wkv), BF16)],
        compiler_params=_cparams(2),
        name="rope_qkv",
    )(u, u, cos2, sin2)


MASKED = -1e30


def _softmax_pv_heads(s, bias, sink_ref, kv, vals, o_ref, col0=0):
    rows = s.shape[0] // GQ
    for g in range(GQ):
        sg = s[g * rows:(g + 1) * rows] * (HD ** -0.5)
        if bias is not None:
            sg = sg + bias
        sink = sink_ref[kv * GQ + g]
        m = jnp.maximum(jnp.max(sg, axis=-1, keepdims=True), sink)
        e = jnp.exp(sg - m)
        denom = jnp.sum(e, axis=-1, keepdims=True) + jnp.exp(sink - m)
        cols = slice(col0 + g * HD, col0 + (g + 1) * HD)
        o_ref[:, cols] = (_dot(e.astype(BF16), vals) * (1.0 / denom)).astype(o_ref.dtype)


def _attn_latent_kernel(sink_ref, bias_ref, q_ref, kl_ref, km_ref, kr_ref, vl_ref, vm_ref, vr_ref, kc_ref,
                        vc_ref, o_ref):
    bias = bias_ref[...]
    for kv in range(HKV):
        hd = slice(kv * HD, (kv + 1) * HD)
        q = jnp.concatenate([q_ref[:, (kv * GQ + g) * HD:(kv * GQ + g + 1) * HD] for g in range(GQ)], axis=0)
        keys = jnp.concatenate([kl_ref[:, hd], km_ref[:, hd], kr_ref[:, hd], kc_ref[:, hd].astype(BF16)], axis=0)
        vals = jnp.concatenate([vl_ref[:, hd], vm_ref[:, hd], vr_ref[:, hd], vc_ref[:, hd].astype(BF16)], axis=0)
        _softmax_pv_heads(_dot_nt(q, keys), bias, sink_ref, kv, vals, o_ref, col0=kv * GQ * HD)


def _band_bias(t_len, tc):
    bq = BLOCK_Q
    r = jnp.arange(bq)[:, None]
    c = jnp.arange(3 * bq)[None, :]
    in_band = jnp.abs(r - (c - bq)) <= WINDOW
    kinds = []
    for first, last in ((False, False), (True, False), (False, True), (True, True)):
        ok = in_band
        if first:
            ok = ok & (c >= bq)
        if last:
            ok = ok & (c < 2 * bq)
        kinds.append(jnp.concatenate([jnp.where(ok, 0.0, MASKED), jnp.zeros((bq, tc))], axis=1))
    return jnp.stack(kinds).astype(F32)


def _attention_latent(sink, q_r, kv_r, uc):
    b, t, _ = q_r.shape
    tc = uc.shape[1]
    nb = t // BLOCK_Q
    wq, wkv = HQ * HD, HKV * HD
    nkeys = 3 * BLOCK_Q + tc
    kblk = lambda off: pl.BlockSpec(
        (None, BLOCK_Q, wkv), lambda bi, n: (bi, jnp.clip(n + off, 0, nb - 1), 0))
    vblk = lambda off: pl.BlockSpec(
        (None, BLOCK_Q, wkv), lambda bi, n: (bi, jnp.clip(n + off, 0, nb - 1), 1))
    kind = lambda n: (n == 0).astype(jnp.int32) + 2 * (n == nb - 1).astype(jnp.int32)
    return pl.pallas_call(
        _attn_latent_kernel,
        grid=(b, nb),
        in_specs=[
            pl.BlockSpec(memory_space=pltpu.SMEM),
            pl.BlockSpec((None, BLOCK_Q, nkeys), lambda bi, n: (kind(n), 0, 0)),
            pl.BlockSpec((None, BLOCK_Q, wq), lambda bi, n: (bi, n, 0)),
            kblk(-1), kblk(0), kblk(1), vblk(-1), vblk(0), vblk(1),
            pl.BlockSpec((None, tc, wkv), lambda bi, n: (bi, 0, COL_CK // wkv)),
            pl.BlockSpec((None, tc, wkv), lambda bi, n: (bi, 0, COL_CV // wkv)),
        ],
        out_specs=pl.BlockSpec((None, BLOCK_Q, wq), lambda bi, n: (bi, n, 0)),
        out_shape=jax.ShapeDtypeStruct((b, t, wq), BF16),
        compiler_params=_cparams(2),
        name="attn_latent",
    )(sink, _band_bias(t, tc), q_r, kv_r, kv_r, kv_r, kv_r, kv_r, kv_r, uc, uc)


def _attn_ctx_kernel(sink_ref, q_ref, kc_ref, vc_ref, o_ref):
    kv = pl.program_id(1)
    q = jnp.concatenate([q_ref[:, g * HD:(g + 1) * HD] for g in range(GQ)], axis=0).astype(BF16)
    s = _dot_nt(q, kc_ref[...].astype(BF16))
    _softmax_pv_heads(s, None, sink_ref, kv, vc_ref[...].astype(BF16), o_ref)


def _attention_ctx(sink, uc):
    b, tc, _ = uc.shape
    wq = GQ * HD
    return pl.pallas_call(
        _attn_ctx_kernel,
        grid=(b, HKV),
        in_specs=[
            pl.BlockSpec(memory_space=pltpu.SMEM),
            pl.BlockSpec((None, tc, wq), lambda bi, kv: (bi, 0, COL_CQ // wq + kv)),
            pl.BlockSpec((None, tc, HD), lambda bi, kv: (bi, 0, COL_CK // HD + kv)),
            pl.BlockSpec((None, tc, HD), lambda bi, kv: (bi, 0, COL_CV // HD + kv)),
        ],
        out_specs=pl.BlockSpec((None, tc, wq), lambda bi, kv: (bi, 0, kv)),
        out_shape=jax.ShapeDtypeStruct((b, tc, HQ * HD), BF16),
        compiler_params=_cparams(2),
        name="attn_ctx",
    )(sink, uc, uc, uc)


def _pool_kernel(u_ref, w_ref, sc_ref, o_ref):
    t_len = u_ref.shape[0]
    grp = pl.program_id(1)
    x = u_ref[...]
    t = lax.broadcasted_iota(jnp.int32, x.shape, 0)

    def shift_down(a, k):
        return jnp.where(t >= k, pltpu.roll(a, k, 0), 0.0)

    def shift_up(a, k):
        return jnp.where(t < t_len - k, pltpu.roll(a, t_len - k, 0), 0.0)

    for j, win in enumerate(POOL_WINDOWS):
        @pl.when(grp == j)
        def _(win=win):
            half = win // 2
            trail, lead, step = x, x, 1
            while step < half:
                trail = trail + shift_down(trail, step)
                lead = lead + shift_up(lead, step)
                step *= 2
            total = shift_down(trail, 1) + lead
            cnt = (jnp.minimum(t + half, t_len) - jnp.maximum(t - half, 0)).astype(F32)
            y = _dot((total / cnt - x).astype(BF16), w_ref[...])
            o_ref[...] = (y * sc_ref[...]).astype(o_ref.dtype)


def _pool_mixer(u, pool_w, layer, pool_scale):
    b, t, _ = u.shape
    ng = len(POOL_WINDOWS)
    return pl.pallas_call(
        _pool_kernel,
        grid=(b, ng),
        in_specs=[
            pl.BlockSpec((None, t, POOL_GW), lambda bi, g: (bi, 0, COL_B // POOL_GW + g)),
            pl.BlockSpec((None, None, POOL_GW, POOL_GW), lambda bi, g: (layer, g, 0, 0)),
            pl.BlockSpec((1, POOL_GW), lambda bi, g: (0, g)),
        ],
        out_specs=pl.BlockSpec((None, t, POOL_GW), lambda bi, g: (bi, 0, g)),
        out_shape=jax.ShapeDtypeStruct((b, t, ng * POOL_GW), BF16),
        compiler_params=_cparams(2),
        name="pool_mixer",
    )(u, pool_w, pool_scale)


SAFE_LOG_DECAY = 80.0
F32_TINY = 1.1754944e-38
DIAG_CODE, OFF_CODE = 2.0, 1.0


def _hgrn_pair_codes(tb, rev):
    r = jnp.arange(tb)[:, None]
    c = jnp.arange(tb)[None, :]
    ordered = (c >= r) if rev else (c <= r)
    same_chunk = (r // CHUNK) == (c // CHUNK)
    same_sub = (r // SUB) == (c // SUB)
    return jnp.where(same_sub & ordered, DIAG_CODE, jnp.where(same_chunk & ordered, OFF_CODE, 0.0)).astype(F32)


def _hgrn_kernel(*refs, rev, final, has_init, tb):
    refs = list(refs)
    q_ref, i_ref, f_ref, gconst_ref, code_ref = refs[:5]
    del refs[:5]
    s0_ref = refs.pop(0) if has_init else None
    if final:
        g_ref, ob_ref, ng_ref = refs[:3]
        del refs[:3]
    o_ref, s_out_ref, st_ref, k_ref, bsub_ref, rest_ref = refs
    step = pl.program_id(2)
    n_chunks = tb // CHUNK
    n_sub = CHUNK // SUB
    heads = st_ref.shape[0]

    @pl.when(step == 0)
    def _():
        st_ref[...] = s0_ref[...] if has_init else jnp.zeros_like(st_ref)

    code = code_ref[...]
    row = lax.broadcasted_iota(jnp.int32, (tb, LANES), 0)
    pos_in_sub = row % SUB
    sub_of_row = (row % CHUNK) // SUB
    chunk_of_row = row // CHUNK
    sub_order = list(reversed(range(n_sub))) if rev else list(range(n_sub))
    chunk_order = list(reversed(range(n_chunks))) if rev else list(range(n_chunks))

    def rows_of(values, height):
        return jnp.concatenate([jnp.broadcast_to(val, (height, LANES)) for val in values], axis=0)

    def diag_exact(q, k, v, b_sub):
        pos = pos_in_sub
        acc = jnp.zeros((tb, LANES), F32)
        for d in range(SUB):
            if d == 0:
                kd, bd, vd = k, b_sub, v
            else:
                sh = tb - d if rev else d
                kd, bd, vd = pltpu.roll(k, sh, 0), pltpu.roll(b_sub, sh, 0), pltpu.roll(v, sh, 0)
            a = jnp.sum(q * kd * jnp.exp(jnp.minimum(b_sub - bd, 0.0)), axis=-1, keepdims=True)
            valid = (pos <= SUB - 1 - d) if rev else (pos >= d)
            acc = acc + jnp.where(valid, a, 0.0) * vd
        return acc

    def finish(o, cols):
        if final:
            o = o + ob_ref[:, cols]
            g = g_ref[:, cols]
            o = o * lax.rsqrt(jnp.mean(jnp.square(o), axis=-1, keepdims=True) + RMS_EPS)
            o = o * ng_ref[:, cols] * (g * jax.nn.sigmoid(g))
        return o.astype(o_ref.dtype)

    worst = jnp.zeros((), F32)
    for hh in range(heads):
        cols = slice(hh * LANES, (hh + 1) * LANES)
        q = q_ref[:, cols]
        v = i_ref[:, cols]
        z = f_ref[:, cols]
        lb = gconst_ref[0:1, cols]
        one_m_lb = gconst_ref[1:2, cols]

        e = jnp.exp(-jnp.abs(z))
        r = 1.0 / (1.0 + e)
        num = jnp.where(z >= 0.0, 1.0 + lb * e, lb + e)
        logf = jnp.log(jnp.maximum(num * r, F32_TINY))
        k = one_m_lb * jnp.where(z >= 0.0, e * r, r)
        v16 = v.astype(BF16)

        b_sub = logf
        for s in (1, 2, 4, 8):
            if rev:
                b_sub = b_sub + jnp.where(pos_in_sub < SUB - s, pltpu.roll(b_sub, tb - s, 0), 0.0)
            else:
                b_sub = b_sub + jnp.where(pos_in_sub >= s, pltpu.roll(b_sub, s, 0), 0.0)
        last = 0 if rev else SUB - 1
        zero_row = jnp.zeros((1, LANES), F32)
        before_sub = [None] * (n_chunks * n_sub)
        before_chunk = [None] * n_chunks
        run_block = zero_row
        for c in chunk_order:
            run = zero_row
            for j in sub_order:
                u = c * n_sub + j
                before_sub[u] = run
                run = run + b_sub[u * SUB + last:u * SUB + last + 1]
            before_chunk[c] = run_block
            run_block = run_block + run
        b_chunk = b_sub + rows_of(before_sub, SUB)
        b_blk = b_chunk + rows_of(before_chunk, CHUNK)

        q_sub = q * jnp.exp(b_sub)
        q_parts, k_parts = [], []
        for i in range(1, n_sub):
            j = sub_order[i]
            b_ref = rows_of([before_sub[c * n_sub + j] for c in range(n_chunks)], CHUNK)
            is_src = (sub_of_row > j) if rev else (sub_of_row < j)
            q_parts.append(jnp.where(sub_of_row == j, q_sub, 0.0).astype(BF16))
            k_parts.append(jnp.where(is_src, k * jnp.exp(jnp.minimum(b_ref - b_chunk, 0.0)), 0.0).astype(BF16))
        s_off = _dot_nt(jnp.concatenate(q_parts, axis=1), jnp.concatenate(k_parts, axis=1))
        a_off = jnp.where(code == OFF_CODE, s_off, 0.0)

        if n_chunks > 1:
            q_chunk = q * jnp.exp(b_chunk)
            q_parts, k_parts = [], []
            for i in range(1, n_chunks):
                c = chunk_order[i]
                is_src = (chunk_of_row > c) if rev else (chunk_of_row < c)
                q_parts.append(jnp.where(chunk_of_row == c, q_chunk, 0.0).astype(BF16))
                k_parts.append(
                    jnp.where(is_src, k * jnp.exp(jnp.minimum(before_chunk[c] - b_blk, 0.0)), 0.0).astype(BF16))
            a_off = a_off + _dot_nt(jnp.concatenate(q_parts, axis=1), jnp.concatenate(k_parts, axis=1))
        o = _dot(a_off.astype(BF16), v16)

        st = st_ref[hh]
        o = o + _dot_nt((q * jnp.exp(b_blk)).astype(BF16), st.astype(BF16))
        k_out = (k * jnp.exp(run_block - b_blk)).astype(BF16)
        st_ref[hh] = st * jnp.exp(run_block) + _dot_tn(v16, k_out)

        k_ref[:, cols] = k
        bsub_ref[:, cols] = b_sub
        rest_ref[:, cols] = o

        k_sub = (k * jnp.exp(-b_sub)).astype(BF16)
        s_diag = _dot_nt(q_sub.astype(BF16), k_sub)
        o = o + _dot(jnp.where(code == DIAG_CODE, s_diag, 0.0).astype(BF16), v16)
        worst = jnp.maximum(worst, jnp.max(-b_sub))
        o_ref[:, cols] = finish(o, cols)

    @pl.when(worst > SAFE_LOG_DECAY)
    def _():
        for hh in range(heads):
            cols = slice(hh * LANES, (hh + 1) * LANES)
            exact = diag_exact(q_ref[:, cols], k_ref[:, cols], i_ref[:, cols], bsub_ref[:, cols])
            o_ref[:, cols] = finish(rest_ref[:, cols] + exact, cols)

    @pl.when(step == pl.num_programs(2) - 1)
    def _():
        s_out_ref[...] = st_ref[...]


def _hgrn_scan(u, gconst, rev, init_state=None, final_inputs=None):
    b, t, _ = u.shape
    tb = min(256, t)
    nblk = t // tb
    final = final_inputs is not None
    has_init = init_state is not None
    code = _hgrn_pair_codes(tb, rev)
    heads = HGRN_HEADS_PER_STEP
    hw = heads * LANES
    row = (lambda s: nblk - 1 - s) if rev else (lambda s: s)

    def col_spec(col0):
        return pl.BlockSpec((None, tb, hw), lambda bi, h, s: (bi, row(s), col0 // hw + h))

    state_spec = pl.BlockSpec((None, heads, LANES, LANES), lambda bi, h, s: (bi, h, 0, 0))
    in_specs = [col_spec(COL_AQ), col_spec(COL_AI), col_spec(COL_AFB if rev else COL_AFF),
                pl.BlockSpec((2, hw), lambda bi, h, s: (0, h)),
                pl.BlockSpec((tb, tb), lambda bi, h, s: (0, 0))]
    args = [u, u, u, gconst, code]
    if has_init:
        in_specs.append(state_spec)
        args.append(init_state)
    if final:
        other, norm_g = final_inputs
        in_specs += [col_spec(COL_AG), col_spec(0), pl.BlockSpec((1, hw), lambda bi, h, s: (0, h))]
        args += [u, other, norm_g]
    return pl.pallas_call(
        functools.partial(_hgrn_kernel, rev=rev, final=final, has_init=has_init, tb=tb),
        grid=(b, HA // heads, nblk),
        in_specs=in_specs,
        out_specs=[col_spec(0), state_spec],
        out_shape=[jax.ShapeDtypeStruct((b, t, BRANCH_W), BF16 if final else F32),
                   jax.ShapeDtypeStruct((b, HA, LANES, LANES), F32)],
        scratch_shapes=[pltpu.VMEM((heads, LANES, LANES), F32),
                        pltpu.VMEM((tb, hw), F32), pltpu.VMEM((tb, hw), F32), pltpu.VMEM((tb, hw), F32)],
        compiler_params=_cparams(3),
        name="hgrn_fwd" if final else "hgrn_bwd",
    )(*args)


def _hgrn_mixer(uc, ul, gconst, norm_g, need_ctx):
    ob_c, sb = _hgrn_scan(uc, gconst[1], rev=True)
    ob_l, _ = _hgrn_scan(ul, gconst[1], rev=True, init_state=sb)
    if need_ctx:
        ya_c, sf = _hgrn_scan(uc, gconst[0], rev=False, final_inputs=(ob_c, norm_g))
    else:
        ya_c = None
        _, sf = _hgrn_scan(uc, gconst[0], rev=False)
    ya_l, _ = _hgrn_scan(ul, gconst[0], rev=False, init_state=sf, final_inputs=(ob_l, norm_g))
    return ya_c, ya_l


def _merge_kernel(ya_ref, yb_ref, yc_ref, wb_ref, g0_ref, g1_ref, g2_ref, o_ref):
    acc = jax.nn.sigmoid(g0_ref[...].astype(F32)) * _dot(ya_ref[...], wb_ref[0])
    acc = acc + jax.nn.sigmoid(g1_ref[...].astype(F32)) * _dot(yb_ref[...], wb_ref[1])
    acc = acc + jax.nn.sigmoid(g2_ref[...].astype(F32)) * _dot(yc_ref[...], wb_ref[2])
    o_ref[...] = acc.astype(o_ref.dtype)


def _merge_branches(gates, gate_col0, ya, yb, yc, w_branch, layer, tm):
    b, t, _ = gates.shape
    d = w_branch.shape[3]
    tn = 512
    assert gate_col0 % tn == 0
    y_spec = pl.BlockSpec((None, tm, BRANCH_W), lambda bi, i, j: (bi, i, 0))
    gate = lambda br: pl.BlockSpec(
        (None, tm, tn), lambda bi, i, j: (bi, i, (gate_col0 + br * d) // tn + j))
    return pl.pallas_call(
        _merge_kernel,
        grid=(b, t // tm, d // tn),
        in_specs=[y_spec, y_spec, y_spec,
                  pl.BlockSpec((None, N_BRANCH, BRANCH_W, tn), lambda bi, i, j: (layer, 0, 0, j)),
                  gate(0), gate(1), gate(2)],
        out_specs=pl.BlockSpec((None, tm, tn), lambda bi, i, j: (bi, i, j)),
        out_shape=jax.ShapeDtypeStruct((b, t, d), BF16),
        compiler_params=_cparams(3),
        name="merge_branches",
    )(ya, yb, yc, w_branch, gates, gates, gates)


def _mmln_kernel(a_ref, w_ref, x_ref, mod_ref, g_ref, b_ref, o_ref, y_ref, *, gate_idx, nj, tn, alpha):
    if nj == 1:
        half = a_ref.shape[0] // 2
        for rows in (slice(0, half), slice(half, 2 * half)):
            res = alpha * x_ref[rows, :] + mod_ref[gate_idx:gate_idx + 1, :] * _dot(a_ref[rows, :], w_ref[...])
            cen = res - jnp.mean(res, axis=-1, keepdims=True)
            var = jnp.mean(jnp.square(cen), axis=-1, keepdims=True)
            o_ref[rows, :] = cen * lax.rsqrt(var + LN_EPS) * g_ref[...] + b_ref[...]
        return

    j = pl.program_id(2)
    y_ref[j] = _dot(a_ref[...], w_ref[...])

    @pl.when(j == nj - 1)
    def _():
        _residual_ln_slabs(y_ref, x_ref, mod_ref, g_ref, b_ref, o_ref, gate_idx=gate_idx, alpha=alpha)


def _residual_ln_slabs(y_ref, x_ref, mod_ref, g_ref, b_ref, o_ref, *, gate_idx, alpha):
    nj, rows, tn = y_ref.shape
    d = nj * tn
    total = jnp.zeros((rows, 1), F32)
    for jj in range(nj):
        cols = slice(jj * tn, (jj + 1) * tn)
        res = alpha * x_ref[:, cols] + mod_ref[gate_idx:gate_idx + 1, cols] * y_ref[jj]
        y_ref[jj] = res
        total = total + jnp.sum(res, axis=-1, keepdims=True)
    mu = total / d
    sq = jnp.zeros_like(total)
    for jj in range(nj):
        sq = sq + jnp.sum(jnp.square(y_ref[jj] - mu), axis=-1, keepdims=True)
    inv = lax.rsqrt(sq / d + LN_EPS)
    for jj in range(nj):
        cols = slice(jj * tn, (jj + 1) * tn)
        o_ref[:, cols] = (y_ref[jj] - mu) * inv * g_ref[:, cols] + b_ref[:, cols]


def _matmul_res_ln(a, w, layer, x, mod, ln_g, ln_b, *, ctx_row, gate_idx, alpha, tm, tn, name):
    b, t, kdim = a.shape
    d = w.shape[2]
    nj = d // tn
    row = (lambda bi: MOD_ROWS // 2) if ctx_row else (lambda bi: bi)
    return pl.pallas_call(
        functools.partial(_mmln_kernel, gate_idx=gate_idx, nj=nj, tn=tn, alpha=alpha),
        grid=(b, t // tm, nj),
        in_specs=[
            pl.BlockSpec((None, tm, kdim), lambda bi, i, j: (bi, i, 0)),
            pl.BlockSpec((None, kdim, tn), lambda bi, i, j: (layer, 0, j)),
            pl.BlockSpec((None, tm, d), lambda bi, i, j: (bi, i, 0)),
            pl.BlockSpec((None, N_ADA, d), lambda bi, i, j: (row(bi), 0, 0)),
            pl.BlockSpec((1, d), lambda bi, i, j: (0, 0)),
            pl.BlockSpec((1, d), lambda bi, i, j: (0, 0)),
        ],
        out_specs=pl.BlockSpec((None, tm, d), lambda bi, i, j: (bi, i, 0)),
        out_shape=jax.ShapeDtypeStruct((b, t, d), F32),
        scratch_shapes=[pltpu.VMEM((nj, tm, tn), F32)],
        compiler_params=_cparams(3),
        name=name,
    )(a, w, x, mod, ln_g, ln_b)


def _rope_tables(t_len):
    rows = t_len // GRID_W
    row = jnp.repeat(jnp.arange(rows, dtype=F32), GRID_W)
    col = jnp.tile(jnp.arange(GRID_W, dtype=F32), rows)
    n_freq = HD // 4
    inv_freq = ROPE_THETA ** (-jnp.arange(n_freq, dtype=F32) / n_freq)
    ang = jnp.concatenate([row[:, None] * inv_freq, col[:, None] * inv_freq], -1)
    cos, sin = jnp.cos(ang), jnp.sin(ang)
    return jnp.concatenate([cos, cos], -1), jnp.concatenate([-sin, sin], -1)


def kernel(x, c, ctx, c_ctx, w_ada, b_ada, w_in, hgrn_lb, hgrn_norm, pool_w, pool_scale, attn_sink, w_branch,
           w_out, ln1_g, ln1_b, w_ff1, w_ff2, ln2_g, ln2_b):
    depth = w_ada.shape[0]
    bsz, t_len, d = x.shape
    tc_len = ctx.shape[1]
    assert d == D_MODEL and w_in.shape[2] == N_COLS and bsz <= MOD_ROWS // 2
    alpha = (2.0 * depth) ** 0.25

    lb = jnp.cumsum(jax.nn.softmax(hgrn_lb.astype(F32), axis=0), axis=0)
    lb = lb - lb[0:1]
    gconst = jnp.stack([lb, 1.0 - lb], axis=2)

    cc = jnp.zeros((MOD_ROWS, d), F32).at[:bsz].set(c).at[MOD_ROWS // 2].set(c_ctx)
    mod = _modulation(cc, w_ada, b_ada).reshape(depth, MOD_ROWS, N_ADA, d)
    cos2, sin2 = _rope_tables(t_len)

    n_ctx = bsz * tc_len
    tm_l = min(1024, t_len)
    tm_c = min(1024, n_ctx)
    tm_ff_l = min(512, t_len)
    tm_ff_c = min(512, n_ctx)

    w_in_b, w_br_b, w_out_b = w_in.astype(BF16), w_branch.astype(BF16), w_out.astype(BF16)
    w_ff1_b, w_ff2_b, pool_w_b = w_ff1.astype(BF16), w_ff2.astype(BF16), pool_w.astype(BF16)

    def flat(a):
        return a.reshape(1, n_ctx, a.shape[-1])

    xc, xl = flat(ctx), x
    for l in range(depth):
        need_ctx = l < depth - 1
        mod_l = mod[l]
        g1, b1 = ln1_g[l].reshape(1, d), ln1_b[l].reshape(1, d)
        g2, b2 = ln2_g[l].reshape(1, d), ln2_b[l].reshape(1, d)
        sink_l = attn_sink[l].astype(F32)
        scale_l = pool_scale[l].reshape(1, -1)
        norm_l = hgrn_norm[l].reshape(1, -1)

        in_proj = functools.partial(_mod_matmul, sh_idx=0, sc_idx=1, act=None, out_dtype=F32, tn=768)
        ul = in_proj(xl, mod_l, w_in_b, l, ctx_row=False, tm=tm_l, name="in_proj_latent")
        uc_flat = in_proj(xc, mod_l, w_in_b, l, ctx_row=True, tm=tm_c, n=None if need_ctx else COL_GATE,
                          name="in_proj_ctx")
        uc = uc_flat.reshape(bsz, tc_len, uc_flat.shape[-1])

        ya_c, ya_l = _hgrn_mixer(uc, ul, gconst[l], norm_l, need_ctx)
        yb_l = _pool_mixer(ul, pool_w_b, l, scale_l)
        q_r, kv_r = _rope_qkv(ul, cos2, sin2)
        yc_l = _attention_latent(sink_l, q_r, kv_r, uc)

        def finish_stream(xs, gates, ya, yb, yc, ctx_row, tm, tm_ff, l=l, mod_l=mod_l, g1=g1, b1=b1, g2=g2,
                          b2=b2):
            merged = _merge_branches(gates, COL_GATE, ya, yb, yc, w_br_b, l, tm)
            xs = _matmul_res_ln(merged, w_out_b, l, xs, mod_l, g1, b1, ctx_row=ctx_row, gate_idx=2, alpha=alpha,
                                tm=tm_ff, tn=d, name="out_proj_ln")
            hidden = _mod_matmul(xs, mod_l, w_ff1_b, l, ctx_row=ctx_row, sh_idx=3, sc_idx=4, act="sq_relu",
                                 out_dtype=BF16, tm=tm, tn=1024, name="ffn_up")
            return _matmul_res_ln(hidden, w_ff2_b, l, xs, mod_l, g2, b2, ctx_row=ctx_row, gate_idx=5,
                                  alpha=alpha, tm=tm_ff, tn=256, name="ffn_down_ln")

        xl = finish_stream(xl, ul, ya_l, yb_l, yc_l, False, tm_l, tm_ff_l)
        if need_ctx:
            yb_c = _pool_mixer(uc, pool_w_b, l, scale_l)
            yc_c = _attention_ctx(sink_l, uc)
            xc = finish_stream(xc, uc_flat, flat(ya_c), flat(yb_c), flat(yc_c), True, tm_c, tm_ff_c)
    return xl
```

```python
import functools

import jax
import jax.numpy as jnp
from jax import lax
from jax.experimental import pallas as pl
from jax.experimental.pallas import tpu as pltpu

F32 = jnp.float32
BF16 = jnp.bfloat16

D_MODEL = 2048
GRID_W = 64
HA, DK_A = 8, 128
WK_A = HA * DK_A
POOL_WINDOWS = (2, 4, 8, 16)
POOL_GW = 256
HQ, HKV, HD = 8, 2, 128
GQ = HQ // HKV
WINDOW = 128
BLOCK_Q = 128
ROPE_THETA = 10000.0
N_BRANCH = 3
BRANCH_W = 1024
N_ADA = 6
LN_EPS = 1e-5
RMS_EPS = 1e-6

COL_AQ = 0
COL_AI = COL_AQ + WK_A
COL_AG = COL_AI + BRANCH_W
COL_AFF = COL_AG + BRANCH_W
COL_AFB = COL_AFF + WK_A
COL_B = COL_AFB + WK_A
COL_CQ = COL_B + BRANCH_W
COL_CK = COL_CQ + BRANCH_W
COL_CV = COL_CK + HKV * HD
COL_GATE = COL_CV + HKV * HD
N_COLS = COL_GATE + N_BRANCH * D_MODEL

LANES = 128
VMEM_LIMIT = 56 * 1024 * 1024

CHUNK = 64
SUB = 16
HGRN_HEADS_PER_STEP = 8
MOD_ROWS = 16


def _cparams(n_axes):
    return pltpu.CompilerParams(dimension_semantics=("arbitrary",) * n_axes, vmem_limit_bytes=VMEM_LIMIT)


def _dot(a, b):
    return jnp.dot(a, b, preferred_element_type=F32)


def _dot_nt(a, b):
    return lax.dot_general(a, b, (((1,), (1,)), ((), ())), preferred_element_type=F32)


def _dot_tn(a, b):
    return lax.dot_general(a, b, (((0,), (0,)), ((), ())), preferred_element_type=F32)


def _mod_kernel(c_ref, w_ref, b_ref, o_ref):
    c = c_ref[...]
    s = (c * jax.nn.sigmoid(c)).astype(BF16)
    o_ref[...] = _dot(s, w_ref[...].astype(BF16)) + b_ref[...]


def _modulation(cc, w_ada, b_ada):
    depth, d, n = w_ada.shape
    tn = 1024
    return pl.pallas_call(
        _mod_kernel,
        grid=(depth, n // tn),
        in_specs=[
            pl.BlockSpec((MOD_ROWS, d), lambda l, j: (0, 0)),
            pl.BlockSpec((None, d, tn), lambda l, j: (l, 0, j)),
            pl.BlockSpec((None, 1, tn), lambda l, j: (l, 0, j)),
        ],
        out_specs=pl.BlockSpec((None, MOD_ROWS, tn), lambda l, j: (l, 0, j)),
        out_shape=jax.ShapeDtypeStruct((depth, MOD_ROWS, n), F32),
        compiler_params=_cparams(2),
        name="adaln_mod",
    )(cc, w_ada, b_ada.reshape(depth, 1, n))


def _modmm_kernel(x_ref, mod_ref, w_ref, o_ref, xm_ref, *, sh_idx, sc_idx, act):
    @pl.when(pl.program_id(2) == 0)
    def _():
        sh = mod_ref[sh_idx:sh_idx + 1, :]
        sc = mod_ref[sc_idx:sc_idx + 1, :]
        xm_ref[...] = (x_ref[...] * (1.0 + sc) + sh).astype(BF16)

    y = _dot(xm_ref[...], w_ref[...])
    if act == "sq_relu":
        y = jnp.square(jnp.maximum(y, 0.0))
    o_ref[...] = y.astype(o_ref.dtype)


def _mod_matmul(x, mod, w, layer, *, ctx_row, sh_idx, sc_idx, act, out_dtype, tm, tn, name, col0=0, n=None):
    b, t, d = x.shape
    n = w.shape[2] - col0 if n is None else n
    assert col0 % tn == 0 and n % tn == 0
    row = (lambda bi: MOD_ROWS // 2) if ctx_row else (lambda bi: bi)
    return pl.pallas_call(
        functools.partial(_modmm_kernel, sh_idx=sh_idx, sc_idx=sc_idx, act=act),
        grid=(b, t // tm, n // tn),
        in_specs=[
            pl.BlockSpec((None, tm, d), lambda bi, i, j: (bi, i, 0)),
            pl.BlockSpec((None, N_ADA, d), lambda bi, i, j: (row(bi), 0, 0)),
            pl.BlockSpec((None, d, tn), lambda bi, i, j: (layer, 0, col0 // tn + j)),
        ],
        out_specs=pl.BlockSpec((None, tm, tn), lambda bi, i, j: (bi, i, j)),
        out_shape=jax.ShapeDtypeStruct((b, t, n), out_dtype),
        scratch_shapes=[pltpu.VMEM((tm, d), BF16)],
        compiler_params=_cparams(3),
        name=name,
    )(x, mod, w)


def _rope_kernel(q_ref, kv_ref, cos_ref, sin_ref, qo_ref, kvo_ref):
    cos = cos_ref[...]
    sin = sin_ref[...]

    def rope(v):
        return v * cos + pltpu.roll(v, HD // 2, 1) * sin

    for h in range(HQ):
        sl = slice(h * HD, (h + 1) * HD)
        qo_ref[:, sl] = rope(q_ref[:, sl]).astype(BF16)
    for h in range(HKV):
        sl = slice(h * HD, (h + 1) * HD)
        kvo_ref[:, sl] = rope(kv_ref[:, sl]).astype(BF16)
    kvo_ref[:, HKV * HD:] = kv_ref[:, HKV * HD:].astype(BF16)


def _rope_qkv(u, cos2, sin2):
    b, t, _ = u.shape
    tq = min(512, t)
    wq, wkv = HQ * HD, 2 * HKV * HD
    return pl.pallas_call(
        _rope_kernel,
        grid=(b, t // tq),
        in_specs=[
            pl.BlockSpec((None, tq, wq), lambda bi, i: (bi, i, COL_CQ // wq)),
            pl.BlockSpec((None, tq, wkv), lambda bi, i: (bi, i, COL_CK // wkv)),
            pl.BlockSpec((tq, HD), lambda bi, i: (i, 0)),
            pl.BlockSpec((tq, HD), lambda bi, i: (i, 0)),
        ],
        out_specs=[
            pl.BlockSpec((None, tq, wq), lambda bi, i: (bi, i, 0)),
            pl.BlockSpec((None, tq, wkv), lambda bi, i: (bi, i, 0)),
        ],
        out_shape=[jax.ShapeDtypeStruct((b, t, wq), BF16), jax.ShapeDtypeStruct((b, t, wkv), BF16)],
        compiler_params=_cparams(2),
        name="rope_qkv",
    )(u, u, cos2, sin2)


MASKED = -1e30


ATTN_BLOCKS_PER_STEP = 2


def _softmax_pv_heads(s, bias, sink_ref, kv, vals, o_ref, rows=slice(None), col0=0):
    n = s.shape[0] // GQ
    for g in range(GQ):
        sg = s[g * n:(g + 1) * n] * (HD ** -0.5)
        if bias is not None:
            sg = sg + bias
        sink = sink_ref[kv * GQ + g]
        m = jnp.maximum(jnp.max(sg, axis=-1, keepdims=True), sink)
        e = jnp.exp(sg - m)
        denom = jnp.sum(e, axis=-1, keepdims=True) + jnp.exp(sink - m)
        cols = slice(col0 + g * HD, col0 + (g + 1) * HD)
        o_ref[rows, cols] = (_dot(e.astype(BF16), vals) * (1.0 / denom)).astype(o_ref.dtype)


def _attn_latent_kernel(sink_ref, *refs):
    nq = ATTN_BLOCKS_PER_STEP
    bias_refs, q_ref = refs[:nq], refs[nq]
    k_refs = refs[nq + 1:2 * nq + 3]
    v_refs = refs[2 * nq + 3:3 * nq + 5]
    kc_ref, vc_ref, o_ref = refs[3 * nq + 5:]
    for kv in range(HKV):
        hd = slice(kv * HD, (kv + 1) * HD)
        kc, vc = kc_ref[:, hd].astype(BF16), vc_ref[:, hd].astype(BF16)
        for qb in range(nq):
            rows = slice(qb * BLOCK_Q, (qb + 1) * BLOCK_Q)
            q = jnp.concatenate(
                [q_ref[rows, (kv * GQ + g) * HD:(kv * GQ + g + 1) * HD] for g in range(GQ)], axis=0)
            keys = jnp.concatenate([k_refs[qb + i][:, hd] for i in range(3)] + [kc], axis=0)
            vals = jnp.concatenate([v_refs[qb + i][:, hd] for i in range(3)] + [vc], axis=0)
            _softmax_pv_heads(_dot_nt(q, keys), bias_refs[qb][...], sink_ref, kv, vals, o_ref, rows,
                              col0=kv * GQ * HD)


def _band_bias(t_len, tc):
    bq = BLOCK_Q
    r = jnp.arange(bq)[:, None]
    c = jnp.arange(3 * bq)[None, :]
    in_band = jnp.abs(r - (c - bq)) <= WINDOW
    kinds = []
    for first, last in ((False, False), (True, False), (False, True), (True, True)):
        ok = in_band
        if first:
            ok = ok & (c >= bq)
        if last:
            ok = ok & (c < 2 * bq)
        kinds.append(jnp.concatenate([jnp.where(ok, 0.0, MASKED), jnp.zeros((bq, tc))], axis=1))
    return jnp.stack(kinds).astype(F32)


def _attention_latent(sink, q_r, kv_r, uc):
    b, t, _ = q_r.shape
    tc = uc.shape[1]
    nb = t // BLOCK_Q
    nq = ATTN_BLOCKS_PER_STEP
    assert nb % nq == 0
    wq, wkv = HQ * HD, HKV * HD
    nkeys = 3 * BLOCK_Q + tc
    kv_blk = lambda i, col: pl.BlockSpec(
        (None, BLOCK_Q, wkv), lambda bi, n: (bi, jnp.clip(nq * n - 1 + i, 0, nb - 1), col))
    kind = lambda blk: (blk == 0).astype(jnp.int32) + 2 * (blk == nb - 1).astype(jnp.int32)
    bias_blk = lambda qb: pl.BlockSpec(
        (None, BLOCK_Q, nkeys), lambda bi, n: (kind(nq * n + qb), 0, 0))
    bias = _band_bias(t, tc)
    return pl.pallas_call(
        _attn_latent_kernel,
        grid=(b, nb // nq),
        in_specs=[pl.BlockSpec(memory_space=pltpu.SMEM)]
        + [bias_blk(qb) for qb in range(nq)]
        + [pl.BlockSpec((None, nq * BLOCK_Q, wq), lambda bi, n: (bi, n, 0))]
        + [kv_blk(i, 0) for i in range(nq + 2)] + [kv_blk(i, 1) for i in range(nq + 2)]
        + [pl.BlockSpec((None, tc, wkv), lambda bi, n: (bi, 0, COL_CK // wkv)),
           pl.BlockSpec((None, tc, wkv), lambda bi, n: (bi, 0, COL_CV // wkv))],
        out_specs=pl.BlockSpec((None, nq * BLOCK_Q, wq), lambda bi, n: (bi, n, 0)),
        out_shape=jax.ShapeDtypeStruct((b, t, wq), BF16),
        compiler_params=_cparams(2),
        name="attn_latent",
    )(sink, *([bias] * nq), q_r, *([kv_r] * (2 * nq + 4)), uc, uc)


def _attn_ctx_kernel(sink_ref, q_ref, kc_ref, vc_ref, o_ref):
    kv = pl.program_id(1)
    q = jnp.concatenate([q_ref[:, g * HD:(g + 1) * HD] for g in range(GQ)], axis=0).astype(BF16)
    s = _dot_nt(q, kc_ref[...].astype(BF16))
    _softmax_pv_heads(s, None, sink_ref, kv, vc_ref[...].astype(BF16), o_ref)


def _attention_ctx(sink, uc):
    b, tc, _ = uc.shape
    wq = GQ * HD
    return pl.pallas_call(
        _attn_ctx_kernel,
        grid=(b, HKV),
        in_specs=[
            pl.BlockSpec(memory_space=pltpu.SMEM),
            pl.BlockSpec((None, tc, wq), lambda bi, kv: (bi, 0, COL_CQ // wq + kv)),
            pl.BlockSpec((None, tc, HD), lambda bi, kv: (bi, 0, COL_CK // HD + kv)),
            pl.BlockSpec((None, tc, HD), lambda bi, kv: (bi, 0, COL_CV // HD + kv)),
        ],
        out_specs=pl.BlockSpec((None, tc, wq), lambda bi, kv: (bi, 0, kv)),
        out_shape=jax.ShapeDtypeStruct((b, tc, HQ * HD), BF16),
        compiler_params=_cparams(2),
        name="attn_ctx",
    )(sink, uc, uc, uc)


def _pool_kernel(u_ref, w_ref, sc_ref, o_ref):
    t_len = u_ref.shape[0]
    grp = pl.program_id(1)
    x = u_ref[...]
    t = lax.broadcasted_iota(jnp.int32, x.shape, 0)

    def shift_down(a, k):
        return jnp.where(t >= k, pltpu.roll(a, k, 0), 0.0)

    def shift_up(a, k):
        return jnp.where(t < t_len - k, pltpu.roll(a, t_len - k, 0), 0.0)

    for j, win in enumerate(POOL_WINDOWS):
        @pl.when(grp == j)
        def _(win=win):
            half = win // 2
            trail, lead, step = x, x, 1
            while step < half:
                trail = trail + shift_down(trail, step)
                lead = lead + shift_up(lead, step)
                step *= 2
            total = shift_down(trail, 1) + lead
            cnt = (jnp.minimum(t + half, t_len) - jnp.maximum(t - half, 0)).astype(F32)
            y = _dot((total / cnt - x).astype(BF16), w_ref[...])
            o_ref[...] = (y * sc_ref[...]).astype(o_ref.dtype)


def _pool_mixer(u, pool_w, layer, pool_scale):
    b, t, _ = u.shape
    ng = len(POOL_WINDOWS)
    return pl.pallas_call(
        _pool_kernel,
        grid=(b, ng),
        in_specs=[
            pl.BlockSpec((None, t, POOL_GW), lambda bi, g: (bi, 0, COL_B // POOL_GW + g)),
            pl.BlockSpec((None, None, POOL_GW, POOL_GW), lambda bi, g: (layer, g, 0, 0)),
            pl.BlockSpec((1, POOL_GW), lambda bi, g: (0, g)),
        ],
        out_specs=pl.BlockSpec((None, t, POOL_GW), lambda bi, g: (bi, 0, g)),
        out_shape=jax.ShapeDtypeStruct((b, t, ng * POOL_GW), BF16),
        compiler_params=_cparams(2),
        name="pool_mixer",
    )(u, pool_w, pool_scale)


SAFE_LOG_DECAY = 80.0
F32_TINY = 1.1754944e-38
DIAG_CODE, OFF_CODE = 2.0, 1.0


def _hgrn_pair_codes(tb, rev):
    r = jnp.arange(tb)[:, None]
    c = jnp.arange(tb)[None, :]
    ordered = (c >= r) if rev else (c <= r)
    same_chunk = (r // CHUNK) == (c // CHUNK)
    same_sub = (r // SUB) == (c // SUB)
    return jnp.where(same_sub & ordered, DIAG_CODE, jnp.where(same_chunk & ordered, OFF_CODE, 0.0)).astype(F32)


def _hgrn_kernel(*refs, rev, final, has_init, tb):
    refs = list(refs)
    q_ref, i_ref, f_ref, gconst_ref, code_ref = refs[:5]
    del refs[:5]
    s0_ref = refs.pop(0) if has_init else None
    if final:
        g_ref, ob_ref, ng_ref = refs[:3]
        del refs[:3]
    o_ref, s_out_ref, st_ref, k_ref, bsub_ref, rest_ref = refs
    step = pl.program_id(2)
    n_chunks = tb // CHUNK
    n_sub = CHUNK // SUB
    heads = st_ref.shape[0]

    @pl.when(step == 0)
    def _():
        st_ref[...] = s0_ref[...] if has_init else jnp.zeros_like(st_ref)

    code = code_ref[...]
    row = lax.broadcasted_iota(jnp.int32, (tb, LANES), 0)
    pos_in_sub = row % SUB
    sub_of_row = (row % CHUNK) // SUB
    chunk_of_row = row // CHUNK
    sub_order = list(reversed(range(n_sub))) if rev else list(range(n_sub))
    chunk_order = list(reversed(range(n_chunks))) if rev else list(range(n_chunks))

    def rows_of(values, height):
        return jnp.concatenate([jnp.broadcast_to(val, (height, LANES)) for val in values], axis=0)

    def diag_exact(q, k, v, b_sub):
        pos = pos_in_sub
        acc = jnp.zeros((tb, LANES), F32)
        for d in range(SUB):
            if d == 0:
                kd, bd, vd = k, b_sub, v
            else:
                sh = tb - d if rev else d
                kd, bd, vd = pltpu.roll(k, sh, 0), pltpu.roll(b_sub, sh, 0), pltpu.roll(v, sh, 0)
            a = jnp.sum(q * kd * jnp.exp(jnp.minimum(b_sub - bd, 0.0)), axis=-1, keepdims=True)
            valid = (pos <= SUB - 1 - d) if rev else (pos >= d)
            acc = acc + jnp.where(valid, a, 0.0) * vd
        return acc

    def finish(o, cols):
        if final:
            o = o + ob_ref[:, cols]
            g = g_ref[:, cols]
            o = o * lax.rsqrt(jnp.mean(jnp.square(o), axis=-1, keepdims=True) + RMS_EPS)
            o = o * ng_ref[:, cols] * (g * jax.nn.sigmoid(g))
        return o.astype(o_ref.dtype)

    worst = jnp.zeros((), F32)
    for hh in range(heads):
        cols = slice(hh * LANES, (hh + 1) * LANES)
        q = q_ref[:, cols]
        v = i_ref[:, cols]
        z = f_ref[:, cols]
        lb = gconst_ref[0:1, cols]
        one_m_lb = gconst_ref[1:2, cols]

        e = jnp.exp(-jnp.abs(z))
        r = 1.0 / (1.0 + e)
        num = jnp.where(z >= 0.0, 1.0 + lb * e, lb + e)
        logf = jnp.log(jnp.maximum(num * r, F32_TINY))
        k = one_m_lb * jnp.where(z >= 0.0, e * r, r)
        v16 = v.astype(BF16)

        b_sub = logf
        for s in (1, 2, 4, 8):
            if rev:
                b_sub = b_sub + jnp.where(pos_in_sub < SUB - s, pltpu.roll(b_sub, tb - s, 0), 0.0)
            else:
                b_sub = b_sub + jnp.where(pos_in_sub >= s, pltpu.roll(b_sub, s, 0), 0.0)
        last = 0 if rev else SUB - 1
        zero_row = jnp.zeros((1, LANES), F32)
        before_sub = [None] * (n_chunks * n_sub)
        before_chunk = [None] * n_chunks
        run_block = zero_row
        for c in chunk_order:
            run = zero_row
            for j in sub_order:
                u = c * n_sub + j
                before_sub[u] = run
                run = run + b_sub[u * SUB + last:u * SUB + last + 1]
            before_chunk[c] = run_block
            run_block = run_block + run
        b_chunk = b_sub + rows_of(before_sub, SUB)
        b_blk = b_chunk + rows_of(before_chunk, CHUNK)

        q_sub = q * jnp.exp(b_sub)
        q_parts, k_parts = [], []
        for i in range(1, n_sub):
            j = sub_order[i]
            b_ref = rows_of([before_sub[c * n_sub + j] for c in range(n_chunks)], CHUNK)
            is_src = (sub_of_row > j) if rev else (sub_of_row < j)
            q_parts.append(jnp.where(sub_of_row == j, q_sub, 0.0).astype(BF16))
            k_parts.append(jnp.where(is_src, k * jnp.exp(jnp.minimum(b_ref - b_chunk, 0.0)), 0.0).astype(BF16))
        s_off = _dot_nt(jnp.concatenate(q_parts, axis=1), jnp.concatenate(k_parts, axis=1))
        a_off = jnp.where(code == OFF_CODE, s_off, 0.0)

        if n_chunks > 1:
            q_chunk = q * jnp.exp(b_chunk)
            q_parts, k_parts = [], []
            for i in range(1, n_chunks):
                c = chunk_order[i]
                is_src = (chunk_of_row > c) if rev else (chunk_of_row < c)
                q_parts.append(jnp.where(chunk_of_row == c, q_chunk, 0.0).astype(BF16))
                k_parts.append(
                    jnp.where(is_src, k * jnp.exp(jnp.minimum(before_chunk[c] - b_blk, 0.0)), 0.0).astype(BF16))
            a_off = a_off + _dot_nt(jnp.concatenate(q_parts, axis=1), jnp.concatenate(k_parts, axis=1))
        o = _dot(a_off.astype(BF16), v16)

        st = st_ref[hh]
        o = o + _dot_nt((q * jnp.exp(b_blk)).astype(BF16), st.astype(BF16))
        k_out = (k * jnp.exp(run_block - b_blk)).astype(BF16)
        st_ref[hh] = st * jnp.exp(run_block) + _dot_tn(v16, k_out)

        k_ref[:, cols] = k
        bsub_ref[:, cols] = b_sub
        rest_ref[:, cols] = o

        k_sub = (k * jnp.exp(-b_sub)).astype(BF16)
        s_diag = _dot_nt(q_sub.astype(BF16), k_sub)
        o = o + _dot(jnp.where(code == DIAG_CODE, s_diag, 0.0).astype(BF16), v16)
        worst = jnp.maximum(worst, jnp.max(-b_sub))
        o_ref[:, cols] = finish(o, cols)

    @pl.when(worst > SAFE_LOG_DECAY)
    def _():
        for hh in range(heads):
            cols = slice(hh * LANES, (hh + 1) * LANES)
            exact = diag_exact(q_ref[:, cols], k_ref[:, cols], i_ref[:, cols], bsub_ref[:, cols])
            o_ref[:, cols] = finish(rest_ref[:, cols] + exact, cols)

    @pl.when(step == pl.num_programs(2) - 1)
    def _():
        s_out_ref[...] = st_ref[...]


def _hgrn_scan(u, gconst, rev, init_state=None, final_inputs=None):
    b, t, _ = u.shape
    tb = min(256, t)
    nblk = t // tb
    final = final_inputs is not None
    has_init = init_state is not None
    code = _hgrn_pair_codes(tb, rev)
    heads = HGRN_HEADS_PER_STEP
    hw = heads * LANES
    row = (lambda s: nblk - 1 - s) if rev else (lambda s: s)

    def col_spec(col0):
        return pl.BlockSpec((None, tb, hw), lambda bi, h, s: (bi, row(s), col0 // hw + h))

    state_spec = pl.BlockSpec((None, heads, LANES, LANES), lambda bi, h, s: (bi, h, 0, 0))
    in_specs = [col_spec(COL_AQ), col_spec(COL_AI), col_spec(COL_AFB if rev else COL_AFF),
                pl.BlockSpec((2, hw), lambda bi, h, s: (0, h)),
                pl.BlockSpec((tb, tb), lambda bi, h, s: (0, 0))]
    args = [u, u, u, gconst, code]
    if has_init:
        in_specs.append(state_spec)
        args.append(init_state)
    if final:
        other, norm_g = final_inputs
        in_specs += [col_spec(COL_AG), col_spec(0), pl.BlockSpec((1, hw), lambda bi, h, s: (0, h))]
        args += [u, other, norm_g]
    return pl.pallas_call(
        functools.partial(_hgrn_kernel, rev=rev, final=final, has_init=has_init, tb=tb),
        grid=(b, HA // heads, nblk),
        in_specs=in_specs,
        out_specs=[col_spec(0), state_spec],
        out_shape=[jax.ShapeDtypeStruct((b, t, BRANCH_W), BF16 if final else F32),
                   jax.ShapeDtypeStruct((b, HA, LANES, LANES), F32)],
        scratch_shapes=[pltpu.VMEM((heads, LANES, LANES), F32),
                        pltpu.VMEM((tb, hw), F32), pltpu.VMEM((tb, hw), F32), pltpu.VMEM((tb, hw), F32)],
        compiler_params=_cparams(3),
        name="hgrn_fwd" if final else "hgrn_bwd",
    )(*args)


def _hgrn_mixer(uc, ul, gconst, norm_g, need_ctx):
    ob_c, sb = _hgrn_scan(uc, gconst[1], rev=True)
    ob_l, _ = _hgrn_scan(ul, gconst[1], rev=True, init_state=sb)
    if need_ctx:
        ya_c, sf = _hgrn_scan(uc, gconst[0], rev=False, final_inputs=(ob_c, norm_g))
    else:
        ya_c = None
        _, sf = _hgrn_scan(uc, gconst[0], rev=False)
    ya_l, _ = _hgrn_scan(ul, gconst[0], rev=False, init_state=sf, final_inputs=(ob_l, norm_g))
    return ya_c, ya_l


def _merge_kernel(ya_ref, yb_ref, yc_ref, wb_ref, g0_ref, g1_ref, g2_ref, o_ref):
    acc = jax.nn.sigmoid(g0_ref[...].astype(F32)) * _dot(ya_ref[...], wb_ref[0])
    acc = acc + jax.nn.sigmoid(g1_ref[...].astype(F32)) * _dot(yb_ref[...], wb_ref[1])
    acc = acc + jax.nn.sigmoid(g2_ref[...].astype(F32)) * _dot(yc_ref[...], wb_ref[2])
    o_ref[...] = acc.astype(o_ref.dtype)


def _merge_branches(gates, gate_col0, ya, yb, yc, w_branch, layer, tm):
    b, t, _ = gates.shape
    d = w_branch.shape[3]
    tn = 512
    assert gate_col0 % tn == 0
    y_spec = pl.BlockSpec((None, tm, BRANCH_W), lambda bi, i, j: (bi, i, 0))
    gate = lambda br: pl.BlockSpec(
        (None, tm, tn), lambda bi, i, j: (bi, i, (gate_col0 + br * d) // tn + j))
    return pl.pallas_call(
        _merge_kernel,
        grid=(b, t // tm, d // tn),
        in_specs=[y_spec, y_spec, y_spec,
                  pl.BlockSpec((None, N_BRANCH, BRANCH_W, tn), lambda bi, i, j: (layer, 0, 0, j)),
                  gate(0), gate(1), gate(2)],
        out_specs=pl.BlockSpec((None, tm, tn), lambda bi, i, j: (bi, i, j)),
        out_shape=jax.ShapeDtypeStruct((b, t, d), BF16),
        compiler_params=_cparams(3),
        name="merge_branches",
    )(ya, yb, yc, w_branch, gates, gates, gates)


def _mmln_full_row_kernel(a_ref, w_ref, x_ref, mod_ref, g_ref, b_ref, o_ref, *, gate_idx, alpha):
    half = a_ref.shape[0] // 2
    for rows in (slice(0, half), slice(half, 2 * half)):
        res = alpha * x_ref[rows, :] + mod_ref[gate_idx:gate_idx + 1, :] * _dot(a_ref[rows, :], w_ref[...])
        cen = res - jnp.mean(res, axis=-1, keepdims=True)
        var = jnp.mean(jnp.square(cen), axis=-1, keepdims=True)
        o_ref[rows, :] = cen * lax.rsqrt(var + LN_EPS) * g_ref[...] + b_ref[...]


def _mmln_kernel(a_ref, w_ref, x_ref, mod_ref, g_ref, b_ref, o_ref, y0_ref, y1_ref, *, gate_idx, n_tiles, alpha):
    r = pl.program_id(0)
    j = pl.program_id(1)

    @pl.when(jnp.logical_and(r == 0, j == 0))
    def _():
        y1_ref[...] = jnp.zeros_like(y1_ref)

    for parity, (cur_ref, prev_ref) in enumerate(((y0_ref, y1_ref), (y1_ref, y0_ref))):
        @pl.when(jnp.logical_and(j == 0, r % 2 == parity))
        def _(cur_ref=cur_ref, prev_ref=prev_ref):
            cur_ref[0] = _dot(a_ref[...], w_ref[...])
            _residual_ln_slabs(prev_ref, x_ref, mod_ref, g_ref, b_ref, o_ref, gate_idx=gate_idx, alpha=alpha)

        @pl.when(jnp.logical_and(jnp.logical_and(j > 0, r < n_tiles), r % 2 == parity))
        def _(cur_ref=cur_ref):
            cur_ref[j] = _dot(a_ref[...], w_ref[...])


def _residual_ln_slabs(y_ref, x_ref, mod_ref, g_ref, b_ref, o_ref, *, gate_idx, alpha):
    nj, rows, tn = y_ref.shape
    d = nj * tn
    total = jnp.zeros((rows, 1), F32)
    for jj in range(nj):
        cols = slice(jj * tn, (jj + 1) * tn)
        res = alpha * x_ref[:, cols] + mod_ref[gate_idx:gate_idx + 1, cols] * y_ref[jj]
        y_ref[jj] = res
        total = total + jnp.sum(res, axis=-1, keepdims=True)
    mu = total / d
    sq = jnp.zeros_like(total)
    for jj in range(nj):
        sq = sq + jnp.sum(jnp.square(y_ref[jj] - mu), axis=-1, keepdims=True)
    inv = lax.rsqrt(sq / d + LN_EPS)
    for jj in range(nj):
        cols = slice(jj * tn, (jj + 1) * tn)
        o_ref[:, cols] = (y_ref[jj] - mu) * inv * g_ref[:, cols] + b_ref[:, cols]


def _matmul_res_ln(a, w, layer, x, mod, ln_g, ln_b, *, ctx_row, gate_idx, alpha, tm, tn, name):
    b, t, kdim = a.shape
    d = w.shape[2]
    nj = d // tn
    per_b = t // tm
    n_tiles = b * per_b
    mod_row = (lambda bi: MOD_ROWS // 2) if ctx_row else (lambda bi: bi)
    if nj == 1:
        return pl.pallas_call(
            functools.partial(_mmln_full_row_kernel, gate_idx=gate_idx, alpha=alpha),
            grid=(b, per_b),
            in_specs=[
                pl.BlockSpec((None, tm, kdim), lambda bi, i: (bi, i, 0)),
                pl.BlockSpec((None, kdim, d), lambda bi, i: (layer, 0, 0)),
                pl.BlockSpec((None, tm, d), lambda bi, i: (bi, i, 0)),
                pl.BlockSpec((None, N_ADA, d), lambda bi, i: (mod_row(bi), 0, 0)),
                pl.BlockSpec((1, d), lambda bi, i: (0, 0)),
                pl.BlockSpec((1, d), lambda bi, i: (0, 0)),
            ],
            out_specs=pl.BlockSpec((None, tm, d), lambda bi, i: (bi, i, 0)),
            out_shape=jax.ShapeDtypeStruct((b, t, d), F32),
            compiler_params=_cparams(2),
            name=name,
        )(a, w, x, mod, ln_g, ln_b)

    def cur(r):
        r = jnp.minimum(r, n_tiles - 1)
        return r // per_b, r % per_b

    def prev(r):
        r = jnp.maximum(r - 1, 0)
        return r // per_b, r % per_b

    return pl.pallas_call(
        functools.partial(_mmln_kernel, gate_idx=gate_idx, n_tiles=n_tiles, alpha=alpha),
        grid=(n_tiles + 1, nj),
        in_specs=[
            pl.BlockSpec((None, tm, kdim), lambda r, j: (*cur(r), 0)),
            pl.BlockSpec((None, kdim, tn), lambda r, j: (layer, 0, j)),
            pl.BlockSpec((None, tm, d), lambda r, j: (*prev(r), 0)),
            pl.BlockSpec((None, N_ADA, d), lambda r, j: (mod_row(prev(r)[0]), 0, 0)),
            pl.BlockSpec((1, d), lambda r, j: (0, 0)),
            pl.BlockSpec((1, d), lambda r, j: (0, 0)),
        ],
        out_specs=pl.BlockSpec((None, tm, d), lambda r, j: (*prev(r), 0)),
        out_shape=jax.ShapeDtypeStruct((b, t, d), F32),
        scratch_shapes=[pltpu.VMEM((nj, tm, tn), F32), pltpu.VMEM((nj, tm, tn), F32)],
        compiler_params=_cparams(2),
        name=name,
    )(a, w, x, mod, ln_g, ln_b)


def _rope_tables(t_len):
    rows = t_len // GRID_W
    row = jnp.repeat(jnp.arange(rows, dtype=F32), GRID_W)
    col = jnp.tile(jnp.arange(GRID_W, dtype=F32), rows)
    n_freq = HD // 4
    inv_freq = ROPE_THETA ** (-jnp.arange(n_freq, dtype=F32) / n_freq)
    ang = jnp.concatenate([row[:, None] * inv_freq, col[:, None] * inv_freq], -1)
    cos, sin = jnp.cos(ang), jnp.sin(ang)
    return jnp.concatenate([cos, cos], -1), jnp.concatenate([-sin, sin], -1)


def kernel(x, c, ctx, c_ctx, w_ada, b_ada, w_in, hgrn_lb, hgrn_norm, pool_w, pool_scale, attn_sink, w_branch,
           w_out, ln1_g, ln1_b, w_ff1, w_ff2, ln2_g, ln2_b):
    depth = w_ada.shape[0]
    bsz, t_len, d = x.shape
    tc_len = ctx.shape[1]
    assert d == D_MODEL and w_in.shape[2] == N_COLS and bsz <= MOD_ROWS // 2
    alpha = (2.0 * depth) ** 0.25

    lb = jnp.cumsum(jax.nn.softmax(hgrn_lb.astype(F32), axis=0), axis=0)
    lb = lb - lb[0:1]
    gconst = jnp.stack([lb, 1.0 - lb], axis=2)

    cc = jnp.zeros((MOD_ROWS, d), F32).at[:bsz].set(c).at[MOD_ROWS // 2].set(c_ctx)
    mod = _modulation(cc, w_ada, b_ada).reshape(depth, MOD_ROWS, N_ADA, d)
    cos2, sin2 = _rope_tables(t_len)

    n_ctx = bsz * tc_len
    tm_l = min(1024, t_len)
    tm_c = min(1024, n_ctx)
    tm_ff_l = min(512, t_len)
    tm_ff_c = min(512, n_ctx)

    w_in_b, w_br_b, w_out_b = w_in.astype(BF16), w_branch.astype(BF16), w_out.astype(BF16)
    w_ff1_b, w_ff2_b, pool_w_b = w_ff1.astype(BF16), w_ff2.astype(BF16), pool_w.astype(BF16)

    def flat(a):
        return a.reshape(1, n_ctx, a.shape[-1])

    xc, xl = flat(ctx), x
    for l in range(depth):
        need_ctx = l < depth - 1
        mod_l = mod[l]
        g1, b1 = ln1_g[l].reshape(1, d), ln1_b[l].reshape(1, d)
        g2, b2 = ln2_g[l].reshape(1, d), ln2_b[l].reshape(1, d)
        sink_l = attn_sink[l].astype(F32)
        scale_l = pool_scale[l].reshape(1, -1)
        norm_l = hgrn_norm[l].reshape(1, -1)

        in_proj = functools.partial(_mod_matmul, sh_idx=0, sc_idx=1, act=None, out_dtype=F32, tn=768)
        ul = in_proj(xl, mod_l, w_in_b, l, ctx_row=False, tm=tm_l, name="in_proj_latent")
        uc_flat = in_proj(xc, mod_l, w_in_b, l, ctx_row=True, tm=tm_c, n=None if need_ctx else COL_GATE,
                          name="in_proj_ctx")
        uc = uc_flat.reshape(bsz, tc_len, uc_flat.shape[-1])

        ya_c, ya_l = _hgrn_mixer(uc, ul, gconst[l], norm_l, need_ctx)
        yb_l = _pool_mixer(ul, pool_w_b, l, scale_l)
        q_r, kv_r = _rope_qkv(ul, cos2, sin2)
        yc_l = _attention_latent(sink_l, q_r, kv_r, uc)

        def finish_stream(xs, gates, ya, yb, yc, ctx_row, tm, tm_ff, l=l, mod_l=mod_l, g1=g1, b1=b1, g2=g2,
                          b2=b2):
            merged = _merge_branches(gates, COL_GATE, ya, yb, yc, w_br_b, l, tm)
            xs = _matmul_res_ln(merged, w_out_b, l, xs, mod_l, g1, b1, ctx_row=ctx_row, gate_idx=2, alpha=alpha,
                                tm=tm_ff, tn=d, name="out_proj_ln")
            hidden = _mod_matmul(xs, mod_l, w_ff1_b, l, ctx_row=ctx_row, sh_idx=3, sc_idx=4, act="sq_relu",
                                 out_dtype=BF16, tm=tm, tn=1024, name="ffn_up")
            return _matmul_res_ln(hidden, w_ff2_b, l, xs, mod_l, g2, b2, ctx_row=ctx_row, gate_idx=5,
                                  alpha=alpha, tm=tm_ff, tn=256, name="ffn_down_ln")

        xl = finish_stream(xl, ul, ya_l, yb_l, yc_l, False, tm_l, tm_ff_l)
        if need_ctx:
            yb_c = _pool_mixer(uc, pool_w_b, l, scale_l)
            yc_c = _attention_ctx(sink_l, uc)
            xc = finish_stream(xc, uc_flat, flat(ya_c), flat(yb_c), flat(yc_c), True, tm_c, tm_ff_c)
    return xl
```

```python
import functools

import jax
import jax.numpy as jnp
from jax import lax
from jax.experimental import pallas as pl
from jax.experimental.pallas import tpu as pltpu

F32 = jnp.float32
BF16 = jnp.bfloat16

D_MODEL = 2048
GRID_W = 64
HA, DK_A = 8, 128
WK_A = HA * DK_A
POOL_WINDOWS = (2, 4, 8, 16)
POOL_GW = 256
HQ, HKV, HD = 8, 2, 128
GQ = HQ // HKV
WINDOW = 128
BLOCK_Q = 128
ROPE_THETA = 10000.0
N_BRANCH = 3
BRANCH_W = 1024
N_ADA = 6
LN_EPS = 1e-5
RMS_EPS = 1e-6

COL_AQ = 0
COL_AI = COL_AQ + WK_A
COL_AG = COL_AI + BRANCH_W
COL_AFF = COL_AG + BRANCH_W
COL_AFB = COL_AFF + WK_A
COL_B = COL_AFB + WK_A
COL_CQ = COL_B + BRANCH_W
COL_CK = COL_CQ + BRANCH_W
COL_CV = COL_CK + HKV * HD
COL_GATE = COL_CV + HKV * HD
N_COLS = COL_GATE + N_BRANCH * D_MODEL

LANES = 128
VMEM_LIMIT = 56 * 1024 * 1024

CHUNK = 64
SUB = 16
HGRN_HEADS_PER_STEP = 8
MOD_ROWS = 16


def _cparams(n_axes):
    return pltpu.CompilerParams(dimension_semantics=("arbitrary",) * n_axes, vmem_limit_bytes=VMEM_LIMIT)


def _dot(a, b):
    return jnp.dot(a, b, preferred_element_type=F32)


def _dot_nt(a, b):
    return lax.dot_general(a, b, (((1,), (1,)), ((), ())), preferred_element_type=F32)


def _dot_tn(a, b):
    return lax.dot_general(a, b, (((0,), (0,)), ((), ())), preferred_element_type=F32)


def _mod_kernel(c_ref, w_ref, b_ref, o_ref):
    c = c_ref[...]
    s = (c * jax.nn.sigmoid(c)).astype(BF16)
    o_ref[...] = _dot(s, w_ref[...].astype(BF16)) + b_ref[...]


def _modulation(cc, w_ada, b_ada):
    depth, d, n = w_ada.shape
    tn = 1024
    return pl.pallas_call(
        _mod_kernel,
        grid=(depth, n // tn),
        in_specs=[
            pl.BlockSpec((MOD_ROWS, d), lambda l, j: (0, 0)),
            pl.BlockSpec((None, d, tn), lambda l, j: (l, 0, j)),
            pl.BlockSpec((None, 1, tn), lambda l, j: (l, 0, j)),
        ],
        out_specs=pl.BlockSpec((None, MOD_ROWS, tn), lambda l, j: (l, 0, j)),
        out_shape=jax.ShapeDtypeStruct((depth, MOD_ROWS, n), F32),
        compiler_params=_cparams(2),
        name="adaln_mod",
    )(cc, w_ada, b_ada.reshape(depth, 1, n))


def _modmm_kernel(x_ref, mod_ref, w_ref, o_ref, xm_ref, *, sh_idx, sc_idx, act):
    @pl.when(pl.program_id(2) == 0)
    def _():
        sh = mod_ref[sh_idx:sh_idx + 1, :]
        sc = mod_ref[sc_idx:sc_idx + 1, :]
        xm_ref[...] = (x_ref[...] * (1.0 + sc) + sh).astype(BF16)

    y = _dot(xm_ref[...], w_ref[...])
    if act == "sq_relu":
        y = jnp.square(jnp.maximum(y, 0.0))
    o_ref[...] = y.astype(o_ref.dtype)


def _mod_matmul(x, mod, w, layer, *, ctx_row, sh_idx, sc_idx, act, out_dtype, tm, tn, name, col0=0, n=None):
    b, t, d = x.shape
    n = w.shape[2] - col0 if n is None else n
    assert col0 % tn == 0 and n % tn == 0
    row = (lambda bi: MOD_ROWS // 2) if ctx_row else (lambda bi: bi)
    return pl.pallas_call(
        functools.partial(_modmm_kernel, sh_idx=sh_idx, sc_idx=sc_idx, act=act),
        grid=(b, t // tm, n // tn),
        in_specs=[
            pl.BlockSpec((None, tm, d), lambda bi, i, j: (bi, i, 0)),
            pl.BlockSpec((None, N_ADA, d), lambda bi, i, j: (row(bi), 0, 0)),
            pl.BlockSpec((None, d, tn), lambda bi, i, j: (layer, 0, col0 // tn + j)),
        ],
        out_specs=pl.BlockSpec((None, tm, tn), lambda bi, i, j: (bi, i, j)),
        out_shape=jax.ShapeDtypeStruct((b, t, n), out_dtype),
        scratch_shapes=[pltpu.VMEM((tm, d), BF16)],
        compiler_params=_cparams(3),
        name=name,
    )(x, mod, w)


def _rope_kernel(q_ref, kv_ref, cos_ref, sin_ref, qo_ref, kvo_ref):
    cos = cos_ref[...]
    sin = sin_ref[...]

    def rope(v):
        return v * cos + pltpu.roll(v, HD // 2, 1) * sin

    for h in range(HQ):
        sl = slice(h * HD, (h + 1) * HD)
        qo_ref[:, sl] = rope(q_ref[:, sl]).astype(BF16)
    for h in range(HKV):
        sl = slice(h * HD, (h + 1) * HD)
        kvo_ref[:, sl] = rope(kv_ref[:, sl]).astype(BF16)
    kvo_ref[:, HKV * HD:] = kv_ref[:, HKV * HD:].astype(BF16)


def _rope_qkv(u, cos2, sin2):
    b, t, _ = u.shape
    tq = min(512, t)
    wq, wkv = HQ * HD, 2 * HKV * HD
    return pl.pallas_call(
        _rope_kernel,
        grid=(b, t // tq),
        in_specs=[
            pl.BlockSpec((None, tq, wq), lambda bi, i: (bi, i, COL_CQ // wq)),
            pl.BlockSpec((None, tq, wkv), lambda bi, i: (bi, i, COL_CK // wkv)),
            pl.BlockSpec((tq, HD), lambda bi, i: (i, 0)),
            pl.BlockSpec((tq, HD), lambda bi, i: (i, 0)),
        ],
        out_specs=[
            pl.BlockSpec((None, tq, wq), lambda bi, i: (bi, i, 0)),
            pl.BlockSpec((None, tq, wkv), lambda bi, i: (bi, i, 0)),
        ],
        out_shape=[jax.ShapeDtypeStruct((b, t, wq), BF16), jax.ShapeDtypeStruct((b, t, wkv), BF16)],
        compiler_params=_cparams(2),
        name="rope_qkv",
    )(u, u, cos2, sin2)


MASKED = -1e30


ATTN_BLOCKS_PER_STEP = 2


def _softmax_pv_heads(s, bias, sink_ref, kv, vals, o_ref, rows=slice(None), col0=0):
    n = s.shape[0] // GQ
    for g in range(GQ):
        sg = s[g * n:(g + 1) * n] * (HD ** -0.5)
        if bias is not None:
            sg = sg + bias
        sink = sink_ref[kv * GQ + g]
        m = jnp.maximum(jnp.max(sg, axis=-1, keepdims=True), sink)
        e = jnp.exp(sg - m)
        denom = jnp.sum(e, axis=-1, keepdims=True) + jnp.exp(sink - m)
        cols = slice(col0 + g * HD, col0 + (g + 1) * HD)
        o_ref[rows, cols] = (_dot(e.astype(BF16), vals) * (1.0 / denom)).astype(o_ref.dtype)


def _attn_latent_kernel(sink_ref, *refs):
    nq = ATTN_BLOCKS_PER_STEP
    bias_refs, q_ref = refs[:nq], refs[nq]
    k_refs = refs[nq + 1:2 * nq + 3]
    v_refs = refs[2 * nq + 3:3 * nq + 5]
    kc_ref, vc_ref, o_ref = refs[3 * nq + 5:]
    for kv in range(HKV):
        hd = slice(kv * HD, (kv + 1) * HD)
        kc, vc = kc_ref[:, hd].astype(BF16), vc_ref[:, hd].astype(BF16)
        for qb in range(nq):
            rows = slice(qb * BLOCK_Q, (qb + 1) * BLOCK_Q)
            q = jnp.concatenate(
                [q_ref[rows, (kv * GQ + g) * HD:(kv * GQ + g + 1) * HD] for g in range(GQ)], axis=0)
            keys = jnp.concatenate([k_refs[qb + i][:, hd] for i in range(3)] + [kc], axis=0)
            vals = jnp.concatenate([v_refs[qb + i][:, hd] for i in range(3)] + [vc], axis=0)
            _softmax_pv_heads(_dot_nt(q, keys), bias_refs[qb][...], sink_ref, kv, vals, o_ref, rows,
                              col0=kv * GQ * HD)


def _band_bias(t_len, tc):
    bq = BLOCK_Q
    r = jnp.arange(bq)[:, None]
    c = jnp.arange(3 * bq)[None, :]
    in_band = jnp.abs(r - (c - bq)) <= WINDOW
    kinds = []
    for first, last in ((False, False), (True, False), (False, True), (True, True)):
        ok = in_band
        if first:
            ok = ok & (c >= bq)
        if last:
            ok = ok & (c < 2 * bq)
        kinds.append(jnp.concatenate([jnp.where(ok, 0.0, MASKED), jnp.zeros((bq, tc))], axis=1))
    return jnp.stack(kinds).astype(F32)


def _attention_latent(sink, q_r, kv_r, uc):
    b, t, _ = q_r.shape
    tc = uc.shape[1]
    nb = t // BLOCK_Q
    nq = ATTN_BLOCKS_PER_STEP
    assert nb % nq == 0
    wq, wkv = HQ * HD, HKV * HD
    nkeys = 3 * BLOCK_Q + tc
    kv_blk = lambda i, col: pl.BlockSpec(
        (None, BLOCK_Q, wkv), lambda bi, n: (bi, jnp.clip(nq * n - 1 + i, 0, nb - 1), col))
    kind = lambda blk: (blk == 0).astype(jnp.int32) + 2 * (blk == nb - 1).astype(jnp.int32)
    bias_blk = lambda qb: pl.BlockSpec(
        (None, BLOCK_Q, nkeys), lambda bi, n: (kind(nq * n + qb), 0, 0))
    bias = _band_bias(t, tc)
    return pl.pallas_call(
        _attn_latent_kernel,
        grid=(b, nb // nq),
        in_specs=[pl.BlockSpec(memory_space=pltpu.SMEM)]
        + [bias_blk(qb) for qb in range(nq)]
        + [pl.BlockSpec((None, nq * BLOCK_Q, wq), lambda bi, n: (bi, n, 0))]
        + [kv_blk(i, 0) for i in range(nq + 2)] + [kv_blk(i, 1) for i in range(nq + 2)]
        + [pl.BlockSpec((None, tc, wkv), lambda bi, n: (bi, 0, COL_CK // wkv)),
           pl.BlockSpec((None, tc, wkv), lambda bi, n: (bi, 0, COL_CV // wkv))],
        out_specs=pl.BlockSpec((None, nq * BLOCK_Q, wq), lambda bi, n: (bi, n, 0)),
        out_shape=jax.ShapeDtypeStruct((b, t, wq), BF16),
        compiler_params=_cparams(2),
        name="attn_latent",
    )(sink, *([bias] * nq), q_r, *([kv_r] * (2 * nq + 4)), uc, uc)


def _attn_ctx_kernel(sink_ref, q_ref, kc_ref, vc_ref, o_ref):
    kv = pl.program_id(1)
    q = jnp.concatenate([q_ref[:, g * HD:(g + 1) * HD] for g in range(GQ)], axis=0).astype(BF16)
    s = _dot_nt(q, kc_ref[...].astype(BF16))
    _softmax_pv_heads(s, None, sink_ref, kv, vc_ref[...].astype(BF16), o_ref)


def _attention_ctx(sink, uc):
    b, tc, _ = uc.shape
    wq = GQ * HD
    return pl.pallas_call(
        _attn_ctx_kernel,
        grid=(b, HKV),
        in_specs=[
            pl.BlockSpec(memory_space=pltpu.SMEM),
            pl.BlockSpec((None, tc, wq), lambda bi, kv: (bi, 0, COL_CQ // wq + kv)),
            pl.BlockSpec((None, tc, HD), lambda bi, kv: (bi, 0, COL_CK // HD + kv)),
            pl.BlockSpec((None, tc, HD), lambda bi, kv: (bi, 0, COL_CV // HD + kv)),
        ],
        out_specs=pl.BlockSpec((None, tc, wq), lambda bi, kv: (bi, 0, kv)),
        out_shape=jax.ShapeDtypeStruct((b, tc, HQ * HD), BF16),
        compiler_params=_cparams(2),
        name="attn_ctx",
    )(sink, uc, uc, uc)


def _pool_kernel(u_ref, w_ref, sc_ref, o_ref):
    t_len = u_ref.shape[0]
    grp = pl.program_id(1)
    x = u_ref[...]
    t = lax.broadcasted_iota(jnp.int32, x.shape, 0)

    def shift_down(a, k):
        return jnp.where(t >= k, pltpu.roll(a, k, 0), 0.0)

    def shift_up(a, k):
        return jnp.where(t < t_len - k, pltpu.roll(a, t_len - k, 0), 0.0)

    for j, win in enumerate(POOL_WINDOWS):
        @pl.when(grp == j)
        def _(win=win):
            half = win // 2
            trail, lead, step = x, x, 1
            while step < half:
                trail = trail + shift_down(trail, step)
                lead = lead + shift_up(lead, step)
                step *= 2
            total = shift_down(trail, 1) + lead
            cnt = (jnp.minimum(t + half, t_len) - jnp.maximum(t - half, 0)).astype(F32)
            y = _dot((total / cnt - x).astype(BF16), w_ref[...])
            o_ref[...] = (y * sc_ref[...]).astype(o_ref.dtype)


def _pool_mixer(u, pool_w, layer, pool_scale):
    b, t, _ = u.shape
    ng = len(POOL_WINDOWS)
    return pl.pallas_call(
        _pool_kernel,
        grid=(b, ng),
        in_specs=[
            pl.BlockSpec((None, t, POOL_GW), lambda bi, g: (bi, 0, COL_B // POOL_GW + g)),
            pl.BlockSpec((None, None, POOL_GW, POOL_GW), lambda bi, g: (layer, g, 0, 0)),
            pl.BlockSpec((1, POOL_GW), lambda bi, g: (0, g)),
        ],
        out_specs=pl.BlockSpec((None, t, POOL_GW), lambda bi, g: (bi, 0, g)),
        out_shape=jax.ShapeDtypeStruct((b, t, ng * POOL_GW), BF16),
        compiler_params=_cparams(2),
        name="pool_mixer",
    )(u, pool_w, pool_scale)


SAFE_LOG_DECAY = 80.0
F32_TINY = 1.1754944e-38
DIAG_CODE, OFF_CODE = 2.0, 1.0


def _hgrn_pair_codes(tb, rev):
    r = jnp.arange(tb)[:, None]
    c = jnp.arange(tb)[None, :]
    ordered = (c >= r) if rev else (c <= r)
    same_chunk = (r // CHUNK) == (c // CHUNK)
    same_sub = (r // SUB) == (c // SUB)
    return jnp.where(same_sub & ordered, DIAG_CODE, jnp.where(same_chunk & ordered, OFF_CODE, 0.0)).astype(F32)


def _hgrn_kernel(*refs, rev, final, has_init, tb):
    refs = list(refs)
    q_ref, i_ref, f_ref, gconst_ref, code_ref = refs[:5]
    del refs[:5]
    s0_ref = refs.pop(0) if has_init else None
    if final:
        g_ref, ob_ref, ng_ref = refs[:3]
        del refs[:3]
    o_ref, s_out_ref, st_ref, k_ref, bsub_ref, rest_ref = refs
    step = pl.program_id(2)
    n_chunks = tb // CHUNK
    n_sub = CHUNK // SUB
    heads = st_ref.shape[0]

    @pl.when(step == 0)
    def _():
        st_ref[...] = s0_ref[...] if has_init else jnp.zeros_like(st_ref)

    code = code_ref[...]
    row = lax.broadcasted_iota(jnp.int32, (tb, LANES), 0)
    pos_in_sub = row % SUB
    sub_of_row = (row % CHUNK) // SUB
    chunk_of_row = row // CHUNK
    sub_order = list(reversed(range(n_sub))) if rev else list(range(n_sub))
    chunk_order = list(reversed(range(n_chunks))) if rev else list(range(n_chunks))

    def rows_of(values, height):
        return jnp.concatenate([jnp.broadcast_to(val, (height, LANES)) for val in values], axis=0)

    def diag_exact(q, k, v, b_sub):
        pos = pos_in_sub
        acc = jnp.zeros((tb, LANES), F32)
        for d in range(SUB):
            if d == 0:
                kd, bd, vd = k, b_sub, v
            else:
                sh = tb - d if rev else d
                kd, bd, vd = pltpu.roll(k, sh, 0), pltpu.roll(b_sub, sh, 0), pltpu.roll(v, sh, 0)
            a = jnp.sum(q * kd * jnp.exp(jnp.minimum(b_sub - bd, 0.0)), axis=-1, keepdims=True)
            valid = (pos <= SUB - 1 - d) if rev else (pos >= d)
            acc = acc + jnp.where(valid, a, 0.0) * vd
        return acc

    def finish(o, cols):
        if final:
            o = o + ob_ref[:, cols]
            g = g_ref[:, cols]
            o = o * lax.rsqrt(jnp.mean(jnp.square(o), axis=-1, keepdims=True) + RMS_EPS)
            o = o * ng_ref[:, cols] * (g * jax.nn.sigmoid(g))
        return o.astype(o_ref.dtype)

    worst = jnp.zeros((), F32)
    for hh in range(heads):
        cols = slice(hh * LANES, (hh + 1) * LANES)
        q = q_ref[:, cols]
        v = i_ref[:, cols]
        z = f_ref[:, cols]
        lb = gconst_ref[0:1, cols]
        one_m_lb = gconst_ref[1:2, cols]

        e = jnp.exp(-jnp.abs(z))
        r = 1.0 / (1.0 + e)
        num = jnp.where(z >= 0.0, 1.0 + lb * e, lb + e)
        logf = jnp.log(jnp.maximum(num * r, F32_TINY))
        k = one_m_lb * jnp.where(z >= 0.0, e * r, r)
        v16 = v.astype(BF16)

        b_sub = logf
        for s in (1, 2, 4, 8):
            if rev:
                b_sub = b_sub + jnp.where(pos_in_sub < SUB - s, pltpu.roll(b_sub, tb - s, 0), 0.0)
            else:
                b_sub = b_sub + jnp.where(pos_in_sub >= s, pltpu.roll(b_sub, s, 0), 0.0)
        last = 0 if rev else SUB - 1
        zero_row = jnp.zeros((1, LANES), F32)
        before_sub = [None] * (n_chunks * n_sub)
        before_chunk = [None] * n_chunks
        run_block = zero_row
        for c in chunk_order:
            run = zero_row
            for j in sub_order:
                u = c * n_sub + j
                before_sub[u] = run
                run = run + b_sub[u * SUB + last:u * SUB + last + 1]
            before_chunk[c] = run_block
            run_block = run_block + run
        b_chunk = b_sub + rows_of(before_sub, SUB)
        b_blk = b_chunk + rows_of(before_chunk, CHUNK)

        q_sub = q * jnp.exp(b_sub)
        q_parts, k_parts = [], []
        for i in range(1, n_sub):
            j = sub_order[i]
            b_ref = rows_of([before_sub[c * n_sub + j] for c in range(n_chunks)], CHUNK)
            is_src = (sub_of_row > j) if rev else (sub_of_row < j)
            q_parts.append(jnp.where(sub_of_row == j, q_sub, 0.0).astype(BF16))
            k_parts.append(jnp.where(is_src, k * jnp.exp(jnp.minimum(b_ref - b_chunk, 0.0)), 0.0).astype(BF16))
        s_off = _dot_nt(jnp.concatenate(q_parts, axis=1), jnp.concatenate(k_parts, axis=1))
        a_off = jnp.where(code == OFF_CODE, s_off, 0.0)

        if n_chunks > 1:
            q_chunk = q * jnp.exp(b_chunk)
            q_parts, k_parts = [], []
            for i in range(1, n_chunks):
                c = chunk_order[i]
                is_src = (chunk_of_row > c) if rev else (chunk_of_row < c)
                q_parts.append(jnp.where(chunk_of_row == c, q_chunk, 0.0).astype(BF16))
                k_parts.append(
                    jnp.where(is_src, k * jnp.exp(jnp.minimum(before_chunk[c] - b_blk, 0.0)), 0.0).astype(BF16))
            a_off = a_off + _dot_nt(jnp.concatenate(q_parts, axis=1), jnp.concatenate(k_parts, axis=1))
        o = _dot(a_off.astype(BF16), v16)

        st = st_ref[hh]
        o = o + _dot_nt((q * jnp.exp(b_blk)).astype(BF16), st.astype(BF16))
        k_out = (k * jnp.exp(run_block - b_blk)).astype(BF16)
        st_ref[hh] = st * jnp.exp(run_block) + _dot_tn(v16, k_out)

        k_ref[:, cols] = k
        bsub_ref[:, cols] = b_sub
        rest_ref[:, cols] = o

        k_sub = (k * jnp.exp(-b_sub)).astype(BF16)
        s_diag = _dot_nt(q_sub.astype(BF16), k_sub)
        o = o + _dot(jnp.where(code == DIAG_CODE, s_diag, 0.0).astype(BF16), v16)
        worst = jnp.maximum(worst, jnp.max(-b_sub))
        o_ref[:, cols] = finish(o, cols)

    @pl.when(worst > SAFE_LOG_DECAY)
    def _():
        for hh in range(heads):
            cols = slice(hh * LANES, (hh + 1) * LANES)
            exact = diag_exact(q_ref[:, cols], k_ref[:, cols], i_ref[:, cols], bsub_ref[:, cols])
            o_ref[:, cols] = finish(rest_ref[:, cols] + exact, cols)

    @pl.when(step == pl.num_programs(2) - 1)
    def _():
        s_out_ref[...] = st_ref[...]


def _hgrn_scan(u, gconst, rev, init_state=None, final_inputs=None):
    b, t, _ = u.shape
    tb = min(256, t)
    nblk = t // tb
    final = final_inputs is not None
    has_init = init_state is not None
    code = _hgrn_pair_codes(tb, rev)
    heads = HGRN_HEADS_PER_STEP
    hw = heads * LANES
    row = (lambda s: nblk - 1 - s) if rev else (lambda s: s)

    def col_spec(col0):
        return pl.BlockSpec((None, tb, hw), lambda bi, h, s: (bi, row(s), col0 // hw + h))

    state_spec = pl.BlockSpec((None, heads, LANES, LANES), lambda bi, h, s: (bi, h, 0, 0))
    in_specs = [col_spec(COL_AQ), col_spec(COL_AI), col_spec(COL_AFB if rev else COL_AFF),
                pl.BlockSpec((2, hw), lambda bi, h, s: (0, h)),
                pl.BlockSpec((tb, tb), lambda bi, h, s: (0, 0))]
    args = [u, u, u, gconst, code]
    if has_init:
        in_specs.append(state_spec)
        args.append(init_state)
    if final:
        other, norm_g = final_inputs
        in_specs += [col_spec(COL_AG), col_spec(0), pl.BlockSpec((1, hw), lambda bi, h, s: (0, h))]
        args += [u, other, norm_g]
    return pl.pallas_call(
        functools.partial(_hgrn_kernel, rev=rev, final=final, has_init=has_init, tb=tb),
        grid=(b, HA // heads, nblk),
        in_specs=in_specs,
        out_specs=[col_spec(0), state_spec],
        out_shape=[jax.ShapeDtypeStruct((b, t, BRANCH_W), BF16 if final else F32),
                   jax.ShapeDtypeStruct((b, HA, LANES, LANES), F32)],
        scratch_shapes=[pltpu.VMEM((heads, LANES, LANES), F32),
                        pltpu.VMEM((tb, hw), F32), pltpu.VMEM((tb, hw), F32), pltpu.VMEM((tb, hw), F32)],
        compiler_params=_cparams(3),
        name="hgrn_fwd" if final else "hgrn_bwd",
    )(*args)


def _hgrn_mixer(uc, ul, gconst, norm_g, need_ctx):
    ob_c, sb = _hgrn_scan(uc, gconst[1], rev=True)
    ob_l, _ = _hgrn_scan(ul, gconst[1], rev=True, init_state=sb)
    if need_ctx:
        ya_c, sf = _hgrn_scan(uc, gconst[0], rev=False, final_inputs=(ob_c, norm_g))
    else:
        ya_c = None
        _, sf = _hgrn_scan(uc, gconst[0], rev=False)
    ya_l, _ = _hgrn_scan(ul, gconst[0], rev=False, init_state=sf, final_inputs=(ob_l, norm_g))
    return ya_c, ya_l


def _merge_kernel(ya_ref, yb_ref, yc_ref, wb_ref, g0_ref, g1_ref, g2_ref, o_ref):
    acc = jax.nn.sigmoid(g0_ref[...].astype(F32)) * _dot(ya_ref[...], wb_ref[0])
    acc = acc + jax.nn.sigmoid(g1_ref[...].astype(F32)) * _dot(yb_ref[...], wb_ref[1])
    acc = acc + jax.nn.sigmoid(g2_ref[...].astype(F32)) * _dot(yc_ref[...], wb_ref[2])
    o_ref[...] = acc.astype(o_ref.dtype)


def _merge_branches(gates, gate_col0, ya, yb, yc, w_branch, layer, tm):
    b, t, _ = gates.shape
    d = w_branch.shape[3]
    tn = 512
    assert gate_col0 % tn == 0
    y_spec = pl.BlockSpec((None, tm, BRANCH_W), lambda bi, i, j: (bi, i, 0))
    gate = lambda br: pl.BlockSpec(
        (None, tm, tn), lambda bi, i, j: (bi, i, (gate_col0 + br * d) // tn + j))
    return pl.pallas_call(
        _merge_kernel,
        grid=(b, t // tm, d // tn),
        in_specs=[y_spec, y_spec, y_spec,
                  pl.BlockSpec((None, N_BRANCH, BRANCH_W, tn), lambda bi, i, j: (layer, 0, 0, j)),
                  gate(0), gate(1), gate(2)],
        out_specs=pl.BlockSpec((None, tm, tn), lambda bi, i, j: (bi, i, j)),
        out_shape=jax.ShapeDtypeStruct((b, t, d), BF16),
        compiler_params=_cparams(3),
        name="merge_branches",
    )(ya, yb, yc, w_branch, gates, gates, gates)


def _mmln_full_row_kernel(a_ref, w_ref, x_ref, mod_ref, g_ref, b_ref, o_ref, *, gate_idx, alpha):
    half = a_ref.shape[0] // 2
    for rows in (slice(0, half), slice(half, 2 * half)):
        res = alpha * x_ref[rows, :] + mod_ref[gate_idx:gate_idx + 1, :] * _dot(a_ref[rows, :], w_ref[...])
        cen = res - jnp.mean(res, axis=-1, keepdims=True)
        var = jnp.mean(jnp.square(cen), axis=-1, keepdims=True)
        o_ref[rows, :] = cen * lax.rsqrt(var + LN_EPS) * g_ref[...] + b_ref[...]


def _mmln_kernel(a_ref, w_ref, x_ref, mod_ref, g_ref, b_ref, o_ref, y_ref, *, gate_idx, alpha):
    j = pl.program_id(2)
    y_ref[j] = _dot(a_ref[...], w_ref[...])

    @pl.when(j == y_ref.shape[0] - 1)
    def _():
        _residual_ln_slabs(y_ref, x_ref, mod_ref, g_ref, b_ref, o_ref, gate_idx=gate_idx, alpha=alpha)


def _residual_ln_slabs(y_ref, x_ref, mod_ref, g_ref, b_ref, o_ref, *, gate_idx, alpha):
    nj, rows, tn = y_ref.shape
    d = nj * tn
    total = jnp.zeros((rows, 1), F32)
    for jj in range(nj):
        cols = slice(jj * tn, (jj + 1) * tn)
        res = alpha * x_ref[:, cols] + mod_ref[gate_idx:gate_idx + 1, cols] * y_ref[jj]
        y_ref[jj] = res
        total = total + jnp.sum(res, axis=-1, keepdims=True)
    mu = total / d
    sq = jnp.zeros_like(total)
    for jj in range(nj):
        sq = sq + jnp.sum(jnp.square(y_ref[jj] - mu), axis=-1, keepdims=True)
    inv = lax.rsqrt(sq / d + LN_EPS)
    for jj in range(nj):
        cols = slice(jj * tn, (jj + 1) * tn)
        o_ref[:, cols] = (y_ref[jj] - mu) * inv * g_ref[:, cols] + b_ref[:, cols]


def _matmul_res_ln(a, w, layer, x, mod, ln_g, ln_b, *, ctx_row, gate_idx, alpha, tm, tn, name):
    b, t, kdim = a.shape
    d = w.shape[2]
    nj = d // tn
    per_b = t // tm
    mod_row = (lambda bi: MOD_ROWS // 2) if ctx_row else (lambda bi: bi)
    if nj == 1:
        return pl.pallas_call(
            functools.partial(_mmln_full_row_kernel, gate_idx=gate_idx, alpha=alpha),
            grid=(b, per_b),
            in_specs=[
                pl.BlockSpec((None, tm, kdim), lambda bi, i: (bi, i, 0)),
                pl.BlockSpec((None, kdim, d), lambda bi, i: (layer, 0, 0)),
                pl.BlockSpec((None, tm, d), lambda bi, i: (bi, i, 0)),
                pl.BlockSpec((None, N_ADA, d), lambda bi, i: (mod_row(bi), 0, 0)),
                pl.BlockSpec((1, d), lambda bi, i: (0, 0)),
                pl.BlockSpec((1, d), lambda bi, i: (0, 0)),
            ],
            out_specs=pl.BlockSpec((None, tm, d), lambda bi, i: (bi, i, 0)),
            out_shape=jax.ShapeDtypeStruct((b, t, d), F32),
            compiler_params=_cparams(2),
            name=name,
        )(a, w, x, mod, ln_g, ln_b)

    return pl.pallas_call(
        functools.partial(_mmln_kernel, gate_idx=gate_idx, alpha=alpha),
        grid=(b, per_b, nj),
        in_specs=[
            pl.BlockSpec((None, tm, kdim), lambda bi, i, j: (bi, i, 0)),
            pl.BlockSpec((None, kdim, tn), lambda bi, i, j: (layer, 0, j)),
            pl.BlockSpec((None, tm, d), lambda bi, i, j: (bi, i, 0)),
            pl.BlockSpec((None, N_ADA, d), lambda bi, i, j: (mod_row(bi), 0, 0)),
            pl.BlockSpec((1, d), lambda bi, i, j: (0, 0)),
            pl.BlockSpec((1, d), lambda bi, i, j: (0, 0)),
        ],
        out_specs=pl.BlockSpec((None, tm, d), lambda bi, i, j: (bi, i, 0)),
        out_shape=jax.ShapeDtypeStruct((b, t, d), F32),
        scratch_shapes=[pltpu.VMEM((nj, tm, tn), F32)],
        compiler_params=_cparams(3),
        name=name,
    )(a, w, x, mod, ln_g, ln_b)


def _rope_tables(t_len):
    rows = t_len // GRID_W
    row = jnp.repeat(jnp.arange(rows, dtype=F32), GRID_W)
    col = jnp.tile(jnp.arange(GRID_W, dtype=F32), rows)
    n_freq = HD // 4
    inv_freq = ROPE_THETA ** (-jnp.arange(n_freq, dtype=F32) / n_freq)
    ang = jnp.concatenate([row[:, None] * inv_freq, col[:, None] * inv_freq], -1)
    cos, sin = jnp.cos(ang), jnp.sin(ang)
    return jnp.concatenate([cos, cos], -1), jnp.concatenate([-sin, sin], -1)


def kernel(x, c, ctx, c_ctx, w_ada, b_ada, w_in, hgrn_lb, hgrn_norm, pool_w, pool_scale, attn_sink, w_branch,
           w_out, ln1_g, ln1_b, w_ff1, w_ff2, ln2_g, ln2_b):
    depth = w_ada.shape[0]
    bsz, t_len, d = x.shape
    tc_len = ctx.shape[1]
    assert d == D_MODEL and w_in.shape[2] == N_COLS and bsz <= MOD_ROWS // 2
    alpha = (2.0 * depth) ** 0.25

    lb = jnp.cumsum(jax.nn.softmax(hgrn_lb.astype(F32), axis=0), axis=0)
    lb = lb - lb[0:1]
    gconst = jnp.stack([lb, 1.0 - lb], axis=2)

    cc = jnp.zeros((MOD_ROWS, d), F32).at[:bsz].set(c).at[MOD_ROWS // 2].set(c_ctx)
    mod = _modulation(cc, w_ada, b_ada).reshape(depth, MOD_ROWS, N_ADA, d)
    cos2, sin2 = _rope_tables(t_len)

    n_ctx = bsz * tc_len
    tm_l = min(1024, t_len)
    tm_c = min(1024, n_ctx)
    tm_ff_l = min(512, t_len)
    tm_ff_c = min(512, n_ctx)

    w_in_b, w_br_b, w_out_b = w_in.astype(BF16), w_branch.astype(BF16), w_out.astype(BF16)
    w_ff1_b, w_ff2_b, pool_w_b = w_ff1.astype(BF16), w_ff2.astype(BF16), pool_w.astype(BF16)

    def flat(a):
        return a.reshape(1, n_ctx, a.shape[-1])

    xc, xl = flat(ctx), x
    for l in range(depth):
        need_ctx = l < depth - 1
        mod_l = mod[l]
        g1, b1 = ln1_g[l].reshape(1, d), ln1_b[l].reshape(1, d)
        g2, b2 = ln2_g[l].reshape(1, d), ln2_b[l].reshape(1, d)
        sink_l = attn_sink[l].astype(F32)
        scale_l = pool_scale[l].reshape(1, -1)
        norm_l = hgrn_norm[l].reshape(1, -1)

        in_proj = functools.partial(_mod_matmul, sh_idx=0, sc_idx=1, act=None, out_dtype=F32, tn=768)
        ul = in_proj(xl, mod_l, w_in_b, l, ctx_row=False, tm=tm_l, name="in_proj_latent")
        uc_flat = in_proj(xc, mod_l, w_in_b, l, ctx_row=True, tm=tm_c, n=None if need_ctx else COL_GATE,
                          name="in_proj_ctx")
        uc = uc_flat.reshape(bsz, tc_len, uc_flat.shape[-1])

        ya_c, ya_l = _hgrn_mixer(uc, ul, gconst[l], norm_l, need_ctx)
        yb_l = _pool_mixer(ul, pool_w_b, l, scale_l)
        q_r, kv_r = _rope_qkv(ul, cos2, sin2)
        yc_l = _attention_latent(sink_l, q_r, kv_r, uc)

        def finish_stream(xs, gates, ya, yb, yc, ctx_row, tm, tm_ff, l=l, mod_l=mod_l, g1=g1, b1=b1, g2=g2,
                          b2=b2):
            merged = _merge_branches(gates, COL_GATE, ya, yb, yc, w_br_b, l, tm)
            xs = _matmul_res_ln(merged, w_out_b, l, xs, mod_l, g1, b1, ctx_row=ctx_row, gate_idx=2, alpha=alpha,
                                tm=tm_ff, tn=d, name="out_proj_ln")
            hidden = _mod_matmul(xs, mod_l, w_ff1_b, l, ctx_row=ctx_row, sh_idx=3, sc_idx=4, act="sq_relu",
                                 out_dtype=BF16, tm=tm, tn=1024, name="ffn_up")
            return _matmul_res_ln(hidden, w_ff2_b, l, xs, mod_l, g2, b2, ctx_row=ctx_row, gate_idx=5,
                                  alpha=alpha, tm=tm_ff, tn=256, name="ffn_down_ln")

        xl = finish_stream(xl, ul, ya_l, yb_l, yc_l, False, tm_l, tm_ff_l)
        if need_ctx:
            yb_c = _pool_mixer(uc, pool_w_b, l, scale_l)
            yc_c = _attention_ctx(sink_l, uc)
            xc = finish_stream(xc, uc_flat, flat(ya_c), flat(yb_c), flat(yc_c), True, tm_c, tm_ff_c)
    return xl
```
